```python
import math
import jax
import jax.numpy as jnp
from jax import lax
import numpy as np

D_MODEL = 1024
BATCH = 32
SEQ = 2048
DEPTH = 4

GRID_W = 64
CTX_LEN = 256
N_MIXERS = 2
N_CONV_LAYERS = (DEPTH + N_MIXERS - 1) // N_MIXERS
N_ATTN_LAYERS = DEPTH // N_MIXERS
CONV_WIDTH = 31
DA_HEAD_DIM = 64
DA_V_DIM = 2 * DA_HEAD_DIM
DA_HEADS = D_MODEL // DA_V_DIM
DA_QK_WIDTH = DA_HEADS * 2 * DA_HEAD_DIM
DA_SCALE = DA_HEAD_DIM ** -0.5
ROPE_THETA = 10000.0
Q_BLOCK = 128
D_FF = ((8 * D_MODEL // 3 + 127) // 128) * 128
FFN_CONV_WIDTH = 3
EPS = 1e-6

kernel_name = "hybrid_conformer_diffattn_prefix_dit"


def rms_norm(x, g):
    xf = x.astype(jnp.float32)
    y = xf * lax.rsqrt(jnp.mean(jnp.square(xf), axis=-1, keepdims=True) + EPS)
    return (y * g.astype(jnp.float32)).astype(x.dtype)


def layer_norm(x, g, b):
    xf = x.astype(jnp.float32)
    mu = jnp.mean(xf, axis=-1, keepdims=True)
    var = jnp.mean(jnp.square(xf - mu), axis=-1, keepdims=True)
    y = (xf - mu) * lax.rsqrt(var + EPS)
    return (y * g.astype(jnp.float32) + b.astype(jnp.float32)).astype(x.dtype)


def modulate(h, shift, scale):
    return h * (1 + scale) + shift


def dwconv_same(u, w, b):
    k = w.shape[0]
    pad = (k - 1) // 2
    y = lax.conv_general_dilated(u, w[:, None, :].astype(u.dtype), window_strides=(1,),
                                 padding=((pad, pad),), dimension_numbers=("NWC", "WIO", "NWC"),
                                 feature_group_count=u.shape[-1])
    return y + b


def token_dwconv(u, w, b, on_grid):
    if on_grid:
        bsz, n, ch = u.shape
        rows = n // GRID_W
        return dwconv_same(u.reshape(bsz * rows, GRID_W, ch), w, b).reshape(bsz, n, ch)
    return dwconv_same(u, w, b)


def conformer_conv(h, pw1_w, pw1_b, dw_w, dw_b, ln_g, ln_b, pw2_w, pw2_b, on_grid):
    u = h @ pw1_w + pw1_b
    a, g = jnp.split(u, 2, axis=-1)
    u = a * jax.nn.sigmoid(g)
    u = token_dwconv(u, dw_w, dw_b, on_grid)
    u = jax.nn.silu(layer_norm(u, ln_g, ln_b))
    return u @ pw2_w + pw2_b


def conv_ffn(h, w_up, dw_w, dw_b, w_down, on_grid):
    u = h @ w_up
    u = token_dwconv(u, dw_w, dw_b, on_grid)
    a, v = jnp.split(u, 2, axis=-1)
    return (jax.nn.silu(a) * v) @ w_down


def axial_rope_tables(n_tok):
    t = jnp.arange(n_tok)
    row = (t // GRID_W).astype(jnp.float32)
    col = (t % GRID_W).astype(jnp.float32)
    half = DA_HEAD_DIM // 2
    inv = ROPE_THETA ** (-jnp.arange(0, half, 2, dtype=jnp.float32) / half)
    ang_r = row[:, None] * inv
    ang_c = col[:, None] * inv
    ang = jnp.concatenate([ang_r, ang_r, ang_c, ang_c], axis=-1)
    return jnp.cos(ang)[:, None, None, :], jnp.sin(ang)[:, None, None, :]


def apply_axial_rope(x, cos, sin):
    xs = x.reshape(x.shape[:-1] + (2, 2, DA_HEAD_DIM // 4))
    rot = jnp.concatenate([-xs[..., 1:2, :], xs[..., 0:1, :]], axis=-2).reshape(x.shape)
    return x * cos.astype(x.dtype) + rot * sin.astype(x.dtype)


def diff_attend(q, k, v, lam):
    s = jnp.einsum("bqhcd,bkhcd->bhcqk", q, k, preferred_element_type=jnp.float32) * DA_SCALE
    p = jax.nn.softmax(s, axis=-1)
    a = p[:, :, 0] - lam * p[:, :, 1]
    return jnp.einsum("bhqk,bkhe->bqhe", a.astype(v.dtype), v)


def diff_attention(h_l, h_c, wqkv, wo, qn_g, kn_g, lq1, lk1, lq2, lk2, subln_g, lam_init, cos, sin, need_ctx):
    bsz, n_lat, _ = h_l.shape
    n_ctx = h_c.shape[1]
    f32 = jnp.float32
    lam = (jnp.exp(jnp.sum(lq1.astype(f32) * lk1.astype(f32)))
           - jnp.exp(jnp.sum(lq2.astype(f32) * lk2.astype(f32))) + lam_init)

    def qk_heads(p, L, g):
        return rms_norm(p.reshape(bsz, L, DA_HEADS, 2, DA_HEAD_DIM), g)

    p_l = h_l @ wqkv
    q_l = apply_axial_rope(qk_heads(p_l[..., :DA_QK_WIDTH], n_lat, qn_g), cos, sin)
    k_l = apply_axial_rope(qk_heads(p_l[..., DA_QK_WIDTH:2 * DA_QK_WIDTH], n_lat, kn_g), cos, sin)
    v_l = p_l[..., 2 * DA_QK_WIDTH:].reshape(bsz, n_lat, DA_HEADS, DA_V_DIM)

    p_c = h_c @ (wqkv if need_ctx else wqkv[:, DA_QK_WIDTH:])
    if need_ctx:
        q_c = qk_heads(p_c[..., :DA_QK_WIDTH], n_ctx, qn_g)
        p_c = p_c[..., DA_QK_WIDTH:]
    k_c = qk_heads(p_c[..., :DA_QK_WIDTH], n_ctx, kn_g)
    v_c = p_c[..., DA_QK_WIDTH:].reshape(bsz, n_ctx, DA_HEADS, DA_V_DIM)

    k_all = jnp.concatenate([k_c, k_l], axis=1)
    v_all = jnp.concatenate([v_c, v_l], axis=1)
    nb = n_lat // Q_BLOCK
    qb = q_l.reshape(bsz, nb, Q_BLOCK, DA_HEADS, 2, DA_HEAD_DIM).swapaxes(0, 1)
    o_l = lax.map(lambda q: diff_attend(q, k_all, v_all, lam), qb)
    o_l = o_l.swapaxes(0, 1).reshape(bsz, n_lat, DA_HEADS, DA_V_DIM)

    def out(o, L):
        o = rms_norm(o, subln_g) * (1 - lam_init)
        return o.reshape(bsz, L, DA_HEADS * DA_V_DIM) @ wo

    y_l = out(o_l, n_lat)
    y_c = out(diff_attend(q_c, k_c, v_c, lam), n_ctx) if need_ctx else None
    return y_l, y_c


def setup_inputs(seed: int = 0) -> dict:
    key = jax.random.key(seed)
    ks = iter(jax.random.split(key, 32))
    f32 = jnp.float32
    D = D_MODEL
    NC, NA = N_CONV_LAYERS, N_ATTN_LAYERS

    def nrm(shape, scale):
        return jax.random.normal(next(ks), shape, f32) * scale

    return {
        "x": nrm((BATCH, SEQ, D), 1.0),
        "c": nrm((BATCH, D), 1.0),
        "ctx": nrm((BATCH, CTX_LEN, D), 1.0),
        "c_ctx": nrm((D,), 1.0),
        "ada_w": nrm((DEPTH, D, 6 * D), 0.5 * D ** -0.5),
        "ada_b": nrm((DEPTH, 6 * D), 0.01),
        "mix_norm_g": 1.0 + nrm((DEPTH, D), 0.02),
        "ffn_norm_g": 1.0 + nrm((DEPTH, D), 0.02),
        "cv_pw1_w": nrm((NC, D, 2 * D), D ** -0.5),
        "cv_pw1_b": nrm((NC, 2 * D), 0.01),
        "cv_dw_w": nrm((NC, CONV_WIDTH, D), CONV_WIDTH ** -0.5),
        "cv_dw_b": nrm((NC, D), 0.01),
        "cv_ln_g": 1.0 + nrm((NC, D), 0.02),
        "cv_ln_b": nrm((NC, D), 0.01),
        "cv_pw2_w": nrm((NC, D, D), D ** -0.5),
        "cv_pw2_b": nrm((NC, D), 0.01),
        "da_wqkv": nrm((NA, D, 2 * DA_QK_WIDTH + DA_HEADS * DA_V_DIM), D ** -0.5),
        "da_wo": nrm((NA, DA_HEADS * DA_V_DIM, D), (DA_HEADS * DA_V_DIM) ** -0.5),
        "da_qn_g": 1.0 + nrm((NA, DA_HEAD_DIM), 0.02),
        "da_kn_g": 1.0 + nrm((NA, DA_HEAD_DIM), 0.02),
        "da_lq1": nrm((NA, DA_HEAD_DIM), 0.1),
        "da_lk1": nrm((NA, DA_HEAD_DIM), 0.1),
        "da_lq2": nrm((NA, DA_HEAD_DIM), 0.1),
        "da_lk2": nrm((NA, DA_HEAD_DIM), 0.1),
        "da_subln_g": 1.0 + nrm((NA, DA_V_DIM), 0.02),
        "ffn_w_up": nrm((DEPTH, D, 2 * D_FF), D ** -0.5),
        "ffn_dw_w": nrm((DEPTH, FFN_CONV_WIDTH, 2 * D_FF), FFN_CONV_WIDTH ** -0.5),
        "ffn_dw_b": nrm((DEPTH, 2 * D_FF), 0.01),
        "ffn_w_down": nrm((DEPTH, D_FF, D), D_FF ** -0.5),
    }


def reference(x, c, ctx, c_ctx, ada_w, ada_b, mix_norm_g, ffn_norm_g,
              cv_pw1_w, cv_pw1_b, cv_dw_w, cv_dw_b, cv_ln_g, cv_ln_b, cv_pw2_w, cv_pw2_b,
              da_wqkv, da_wo, da_qn_g, da_kn_g, da_lq1, da_lk1, da_lq2, da_lk2, da_subln_g,
              ffn_w_up, ffn_dw_w, ffn_dw_b, ffn_w_down):
    n_lat = x.shape[1]
    cos, sin = axial_rope_tables(n_lat)
    sc = jax.nn.silu(c)
    sc_ctx = jax.nn.silu(c_ctx)
    for i in range(DEPTH):
        last = i == DEPTH - 1
        j = i // N_MIXERS
        is_conv = (i % N_MIXERS) == 0
        mod_l = (sc @ ada_w[i] + ada_b[i])[:, None, :]
        mod_c = sc_ctx @ ada_w[i] + ada_b[i]
        sh_ml, sc_ml, g_ml, sh_fl, sc_fl, g_fl = jnp.split(mod_l, 6, axis=-1)
        sh_mc, sc_mc, g_mc, sh_fc, sc_fc, g_fc = jnp.split(mod_c, 6, axis=-1)

        h_l = modulate(rms_norm(x, mix_norm_g[i]), sh_ml, sc_ml)
        if is_conv:
            cp = (cv_pw1_w[j], cv_pw1_b[j], cv_dw_w[j], cv_dw_b[j], cv_ln_g[j], cv_ln_b[j], cv_pw2_w[j], cv_pw2_b[j])
            y_l = conformer_conv(h_l, *cp, on_grid=True)
            if not last:
                h_c = modulate(rms_norm(ctx, mix_norm_g[i]), sh_mc, sc_mc)
                y_c = conformer_conv(h_c, *cp, on_grid=False)
        else:
            h_c = modulate(rms_norm(ctx, mix_norm_g[i]), sh_mc, sc_mc)
            lam_init = 0.8 - 0.6 * math.exp(-0.3 * i)
            y_l, y_c = diff_attention(h_l, h_c, da_wqkv[j], da_wo[j], da_qn_g[j], da_kn_g[j],
                                      da_lq1[j], da_lk1[j], da_lq2[j], da_lk2[j], da_subln_g[j],
                                      lam_init, cos, sin, need_ctx=not last)

        x = x + g_ml * y_l
        f_l = modulate(rms_norm(x, ffn_norm_g[i]), sh_fl, sc_fl)
        x = x + g_fl * conv_ffn(f_l, ffn_w_up[i], ffn_dw_w[i], ffn_dw_b[i], ffn_w_down[i], on_grid=True)

        if not last:
            ctx = ctx + g_mc * y_c
            f_c = modulate(rms_norm(ctx, ffn_norm_g[i]), sh_fc, sc_fc)
            ctx = ctx + g_fc * conv_ffn(f_c, ffn_w_up[i], ffn_dw_w[i], ffn_dw_b[i], ffn_w_down[i], on_grid=False)
    return x
```

```python
import functools
import math

import jax
import jax.numpy as jnp
from jax import lax
from jax.experimental import pallas as pl
from jax.experimental.pallas import tpu as pltpu

F32 = jnp.float32
BF16 = jnp.bfloat16

D_MODEL = 1024
DEPTH = 4
GRID_W = 64
N_MIXERS = 2
CONV_WIDTH = 31
CONV_HALO = 16
CONV_SEG = 64
DA_HEAD_DIM = 64
DA_V_DIM = 2 * DA_HEAD_DIM
DA_HEADS = D_MODEL // DA_V_DIM
DA_QK_WIDTH = DA_HEADS * 2 * DA_HEAD_DIM
DA_SCALE = DA_HEAD_DIM ** -0.5
ROPE_THETA = 10000.0
D_FF = ((8 * D_MODEL // 3 + 127) // 128) * 128
EPS = 1e-6

LANES = 128
MOD_ROWS = 40
VMEM_LIMIT = 56 * 1024 * 1024


def _params(sem):
    return pltpu.CompilerParams(dimension_semantics=sem, vmem_limit_bytes=VMEM_LIMIT)


def _silu(x):
    return x * jax.nn.sigmoid(x)


def _mod_rms(x, g, shift, scale):
    ms = jnp.mean(x * x, axis=-1, keepdims=True)
    return (x * lax.rsqrt(ms + EPS) * g) * (1.0 + scale) + shift


def _mods_kernel(c_ref, w_ref, b_ref, o_ref):
    s = _silu(c_ref[...])
    o_ref[0] = jnp.dot(s, w_ref[0], preferred_element_type=F32,
                       precision=lax.Precision.HIGHEST) + b_ref[0]


def _mods(c_all, ada_w, ada_b):
    tn = 512
    n6 = ada_w.shape[-1]
    return pl.pallas_call(
        _mods_kernel,
        grid=(DEPTH, n6 // tn),
        in_specs=[pl.BlockSpec((MOD_ROWS, D_MODEL), lambda i, n: (0, 0)),
                  pl.BlockSpec((1, D_MODEL, tn), lambda i, n: (i, 0, n)),
                  pl.BlockSpec((1, 1, tn), lambda i, n: (i, 0, n))],
        out_specs=pl.BlockSpec((1, MOD_ROWS, tn), lambda i, n: (i, 0, n)),
        out_shape=jax.ShapeDtypeStruct((DEPTH, MOD_ROWS, n6), F32),
        compiler_params=_params(("arbitrary", "arbitrary")),
        name="adaln_maps",
    )(c_all, ada_w, ada_b.reshape(DEPTH, 1, n6))


def _mod_spec(tokens_per_batch, tm, n_grid_axes):
    if tokens_per_batch is None:
        row = lambda t: MOD_ROWS - 8
    else:
        tiles_per_batch = tokens_per_batch // tm
        row = lambda t: t // tiles_per_batch
    if n_grid_axes == 1:
        return pl.BlockSpec((1, 1, 6 * D_MODEL), lambda t: (row(t), 0, 0))
    return pl.BlockSpec((1, 1, 6 * D_MODEL), lambda t, j: (row(t), 0, 0))


def _const(shape):
    return pl.BlockSpec(shape, lambda t: (0,) * len(shape))


def _conv_mixer_kernel(x_ref, mod_ref, ng_ref, w1_ref, b1_ref, dww_ref, dwb_ref, lng_ref, lnb_ref,
                       w2_ref, b2_ref, o_ref, pad_ref, y_ref, *, row_len, n_rows):
    d = D_MODEL
    x = x_ref[...]
    mod = mod_ref[0]
    shift, scale, gate = mod[:, 0:d], mod[:, d:2 * d], mod[:, 2 * d:3 * d]
    h = _mod_rms(x, ng_ref[...], shift, scale).astype(BF16)
    u = jnp.dot(h, w1_ref[...], preferred_element_type=F32) + b1_ref[...]
    u = u[:, :d] * jax.nn.sigmoid(u[:, d:])

    zeros = jnp.zeros((n_rows, CONV_HALO, d), F32)
    pad_ref[:, 0:CONV_HALO, :] = zeros
    pad_ref[:, CONV_HALO + row_len:, :] = zeros
    for r in range(n_rows):
        pad_ref[r, CONV_HALO:CONV_HALO + row_len, :] = u[r * row_len:(r + 1) * row_len, :]

    half = (CONV_WIDTH - 1) // 2

    def row_body(r, carry):
        base = pl.multiple_of(r * row_len, CONV_SEG)
        for p in range(row_len // CONV_SEG):
            for c in range(d // LANES):
                cs = slice(c * LANES, (c + 1) * LANES)
                acc = jnp.zeros((CONV_SEG, LANES), F32)
                for k in range(CONV_WIDTH):
                    off = CONV_HALO - half + p * CONV_SEG + k
                    acc = acc + dww_ref[k:k + 1, cs] * pad_ref[r, off:off + CONV_SEG, cs]
                y_ref[pl.ds(base + p * CONV_SEG, CONV_SEG), cs] = acc + dwb_ref[:, cs]
        return carry

    lax.fori_loop(0, n_rows, row_body, 0)

    y = y_ref[...]
    mu = jnp.mean(y, axis=-1, keepdims=True)
    yc = y - mu
    var = jnp.mean(yc * yc, axis=-1, keepdims=True)
    z = _silu(yc * lax.rsqrt(var + EPS) * lng_ref[...] + lnb_ref[...])
    out = jnp.dot(z.astype(BF16), w2_ref[...], preferred_element_type=F32) + b2_ref[...]
    o_ref[...] = x + gate * out


def _conv_mixer(x2, mods_i, ng, w1, b1, dww, dwb, lng, lnb, w2, b2, *, tokens_per_batch, row_len, tm):
    n, d = x2.shape
    n_rows = tm // row_len
    kern = functools.partial(_conv_mixer_kernel, row_len=row_len, n_rows=n_rows)
    tile = pl.BlockSpec((tm, d), lambda t: (t, 0))
    return pl.pallas_call(
        kern,
        grid=(n // tm,),
        in_specs=[tile, _mod_spec(tokens_per_batch, tm, 1), _const((1, d)),
                  _const((d, 2 * d)), _const((1, 2 * d)), _const((CONV_WIDTH, d)), _const((1, d)),
                  _const((1, d)), _const((1, d)), _const((d, d)), _const((1, d))],
        out_specs=tile,
        out_shape=jax.ShapeDtypeStruct((n, d), F32),
        scratch_shapes=[pltpu.VMEM((n_rows, row_len + 2 * CONV_HALO, d), F32),
                        pltpu.VMEM((tm, d), F32)],
        compiler_params=_params(("arbitrary",)),
        name="conv_mixer",
    )(x2, mods_i, ng, w1, b1, dww, dwb, lng, lnb, w2, b2)


def _head_rms(p, g):
    lane = lax.broadcasted_iota(jnp.int32, p.shape, 1)
    lo = lane < DA_HEAD_DIM
    sq = p * p
    s_lo = jnp.sum(jnp.where(lo, sq, 0.0), axis=-1, keepdims=True)
    s_hi = jnp.sum(jnp.where(lo, 0.0, sq), axis=-1, keepdims=True)
    ms = jnp.where(lo, s_lo, s_hi) * (1.0 / DA_HEAD_DIM)
    return p * lax.rsqrt(ms + EPS) * g


def _rope(x, cos, sin_dn, sin_up):
    quarter = DA_HEAD_DIM // 4
    return (x * cos + pltpu.roll(x, LANES - quarter, axis=1) * sin_dn
            + pltpu.roll(x, quarter, axis=1) * sin_up)


def _qkv_kernel(*refs, rope):
    if rope:
        (x_ref, mod_ref, ng_ref, w_ref, qg_ref, kg_ref, cos_ref, sdn_ref, sup_ref,
         q_ref, k_ref, v_ref) = refs
    else:
        x_ref, mod_ref, ng_ref, w_ref, qg_ref, kg_ref, q_ref, k_ref, v_ref = refs
    d = D_MODEL
    mod = mod_ref[0]
    h = _mod_rms(x_ref[...], ng_ref[...], mod[:, 0:d], mod[:, d:2 * d]).astype(BF16)
    p = jnp.dot(h, w_ref[...], preferred_element_type=F32)
    for hd in range(DA_HEADS):
        cs = slice(hd * DA_V_DIM, (hd + 1) * DA_V_DIM)
        q = _head_rms(p[:, cs], qg_ref[...])
        k = _head_rms(p[:, DA_QK_WIDTH + hd * DA_V_DIM:DA_QK_WIDTH + (hd + 1) * DA_V_DIM], kg_ref[...])
        if rope:
            q = _rope(q, cos_ref[...], sdn_ref[...], sup_ref[...])
            k = _rope(k, cos_ref[...], sdn_ref[...], sup_ref[...])
        q_ref[:, cs] = (q * DA_SCALE).astype(BF16)
        k_ref[:, cs] = k.astype(BF16)
    v_ref[...] = p[:, 2 * DA_QK_WIDTH:].astype(BF16)


def _qkv(x2, mods_i, ng, w, qg, kg, tables, *, tokens_per_batch, tm):
    n, d = x2.shape
    rope = tables is not None
    tile = pl.BlockSpec((tm, d), lambda t: (t, 0))
    in_specs = [tile, _mod_spec(tokens_per_batch, tm, 1), _const((1, d)), _const((d, 3 * d)),
                _const((1, DA_V_DIM)), _const((1, DA_V_DIM))]
    args = [x2, mods_i, ng, w, qg, kg]
    if rope:
        tiles_per_batch = tokens_per_batch // tm
        tab = pl.BlockSpec((tm, LANES), lambda t: (t % tiles_per_batch, 0))
        in_specs += [tab, tab, tab]
        args += list(tables)
    out = jax.ShapeDtypeStruct((n, d), BF16)
    return pl.pallas_call(
        functools.partial(_qkv_kernel, rope=rope),
        grid=(n // tm,),
        in_specs=in_specs,
        out_specs=[tile, tile, tile],
        out_shape=[out, out, out],
        compiler_params=_params(("arbitrary",)),
        name="qkv_rope" if rope else "qkv",
    )(*args)


def _attn_kernel(q_ref, k_ref, v_ref, lq1_ref, lk1_ref, lq2_ref, lk2_ref, sg_ref, o_ref, *, lam_init):
    lam = (jnp.exp(jnp.sum(lq1_ref[...] * lk1_ref[...], axis=-1, keepdims=True))
           - jnp.exp(jnp.sum(lq2_ref[...] * lk2_ref[...], axis=-1, keepdims=True)) + lam_init)
    q = q_ref[0]
    k = k_ref[0]
    lane = lax.broadcasted_iota(jnp.int32, q.shape, 1)
    zero = jnp.zeros_like(q)
    nt = (((1,), (1,)), ((), ()))
    s0 = lax.dot_general(jnp.where(lane < DA_HEAD_DIM, q, zero), k, nt, preferred_element_type=F32)
    s1 = lax.dot_general(jnp.where(lane < DA_HEAD_DIM, zero, q), k, nt, preferred_element_type=F32)
    p0 = jnp.exp(s0 - jnp.max(s0, axis=-1, keepdims=True))
    p1 = jnp.exp(s1 - jnp.max(s1, axis=-1, keepdims=True))
    r0 = 1.0 / jnp.sum(p0, axis=-1, keepdims=True)
    r1 = lam / jnp.sum(p1, axis=-1, keepdims=True)
    a = p0 * r0 - p1 * r1
    o = jnp.dot(a.astype(BF16), v_ref[0], preferred_element_type=F32)
    ms = jnp.mean(o * o, axis=-1, keepdims=True)
    o_ref[0] = (o * lax.rsqrt(ms + EPS) * sg_ref[...] * (1.0 - lam_init)).astype(BF16)


def _attention(q, k, v, lq1, lk1, lq2, lk2, sg, *, lam_init, tq):
    b, lq, d = q.shape
    lk = k.shape[1]
    vec = pl.BlockSpec((1, DA_HEAD_DIM), lambda bi, h, t: (0, 0))
    return pl.pallas_call(
        functools.partial(_attn_kernel, lam_init=lam_init),
        grid=(b, DA_HEADS, lq // tq),
        in_specs=[pl.BlockSpec((1, tq, DA_V_DIM), lambda bi, h, t: (bi, t, h)),
                  pl.BlockSpec((1, lk, DA_V_DIM), lambda bi, h, t: (bi, 0, h)),
                  pl.BlockSpec((1, lk, DA_V_DIM), lambda bi, h, t: (bi, 0, h)),
                  vec, vec, vec, vec,
                  pl.BlockSpec((1, DA_V_DIM), lambda bi, h, t: (0, 0))],
        out_specs=pl.BlockSpec((1, tq, DA_V_DIM), lambda bi, h, t: (bi, t, h)),
        out_shape=jax.ShapeDtypeStruct((b, lq, d), BF16),
        compiler_params=_params(("arbitrary", "arbitrary", "arbitrary")),
        name="diff_attention",
    )(q, k, v, lq1, lk1, lq2, lk2, sg)


def _out_proj_kernel(x_ref, o_ref, mod_ref, w_ref, y_ref):
    d = D_MODEL
    gate = mod_ref[0][:, 2 * d:3 * d]
    y_ref[...] = x_ref[...] + gate * jnp.dot(o_ref[...], w_ref[...], preferred_element_type=F32)


def _out_proj(x2, o2, mods_i, w, *, tokens_per_batch, tm):
    n, d = x2.shape
    tile = pl.BlockSpec((tm, d), lambda t: (t, 0))
    return pl.pallas_call(
        _out_proj_kernel,
        grid=(n // tm,),
        in_specs=[tile, tile, _mod_spec(tokens_per_batch, tm, 1), _const((d, d))],
        out_specs=tile,
        out_shape=jax.ShapeDtypeStruct((n, d), F32),
        compiler_params=_params(("arbitrary",)),
        name="attn_out_proj",
    )(x2, o2, mods_i, w)


def _dwconv3(u, w_ref, b_ref, row_len):
    tm = u.shape[0]
    t = lax.broadcasted_iota(jnp.int32, (tm, 1), 0) % row_len
    prev = jnp.where(t == 0, 0.0, pltpu.roll(u, 1, axis=0))
    nxt = jnp.where(t == row_len - 1, 0.0, pltpu.roll(u, tm - 1, axis=0))
    return w_ref[0:1, :] * prev + w_ref[1:2, :] * u + w_ref[2:3, :] * nxt + b_ref[...]


def _ffn_kernel(x_ref, mod_ref, ng_ref, wa_ref, wv_ref, dwa_ref, dwv_ref, ba_ref, bv_ref, wd_ref,
                o_ref, f_ref, acc_ref, *, row_len):
    d = D_MODEL
    j = pl.program_id(1)

    @pl.when(j == 0)
    def _():
        mod = mod_ref[0]
        f_ref[...] = _mod_rms(x_ref[...], ng_ref[...], mod[:, 3 * d:4 * d], mod[:, 4 * d:5 * d]).astype(BF16)
        acc_ref[...] = jnp.zeros_like(acc_ref)

    f = f_ref[...]
    ua = _dwconv3(jnp.dot(f, wa_ref[...], preferred_element_type=F32), dwa_ref, ba_ref, row_len)
    uv = _dwconv3(jnp.dot(f, wv_ref[...], preferred_element_type=F32), dwv_ref, bv_ref, row_len)
    hid = (_silu(ua) * uv).astype(BF16)
    acc_ref[...] += jnp.dot(hid, wd_ref[...], preferred_element_type=F32)

    @pl.when(j == pl.num_programs(1) - 1)
    def _():
        gate = mod_ref[0][:, 5 * d:6 * d]
        o_ref[...] = x_ref[...] + gate * acc_ref[...]


def _ffn(x2, mods_i, ng, w_up, dw_w, dw_b, w_down, *, tokens_per_batch, row_len, tm, tf):
    n, d = x2.shape
    nj = D_FF // tf
    tile = pl.BlockSpec((tm, d), lambda t, j: (t, 0))
    return pl.pallas_call(
        functools.partial(_ffn_kernel, row_len=row_len),
        grid=(n // tm, nj),
        in_specs=[tile, _mod_spec(tokens_per_batch, tm, 2),
                  pl.BlockSpec((1, d), lambda t, j: (0, 0)),
                  pl.BlockSpec((d, tf), lambda t, j: (0, j)),
                  pl.BlockSpec((d, tf), lambda t, j: (0, nj + j)),
                  pl.BlockSpec((3, tf), lambda t, j: (0, j)),
                  pl.BlockSpec((3, tf), lambda t, j: (0, nj + j)),
                  pl.BlockSpec((1, tf), lambda t, j: (0, j)),
                  pl.BlockSpec((1, tf), lambda t, j: (0, nj + j)),
                  pl.BlockSpec((tf, d), lambda t, j: (j, 0))],
        out_specs=tile,
        out_shape=jax.ShapeDtypeStruct((n, d), F32),
        scratch_shapes=[pltpu.VMEM((tm, d), BF16), pltpu.VMEM((tm, d), F32)],
        compiler_params=_params(("arbitrary", "arbitrary")),
        name="conv_ffn",
    )(x2, mods_i, ng, w_up, w_up, dw_w, dw_w, dw_b, dw_b, w_down)


def _rope_tables(n_tok):
    t = jnp.arange(n_tok)
    row = (t // GRID_W).astype(F32)
    col = (t % GRID_W).astype(F32)
    half = DA_HEAD_DIM // 2
    inv = ROPE_THETA ** (-jnp.arange(0, half, 2, dtype=F32) / half)
    ang_r = row[:, None] * inv
    ang_c = col[:, None] * inv
    ang = jnp.concatenate([ang_r, ang_r, ang_c, ang_c], axis=-1)
    ang = jnp.concatenate([ang, ang], axis=-1)
    first = (jnp.arange(LANES) % (2 * (DA_HEAD_DIM // 4))) < DA_HEAD_DIM // 4
    sin = jnp.sin(ang)
    return jnp.cos(ang), jnp.where(first, -sin, 0.0), jnp.where(first, 0.0, sin)


def kernel(x, c, ctx, c_ctx, ada_w, ada_b, mix_norm_g, ffn_norm_g, cv_pw1_w, cv_pw1_b, cv_dw_w, cv_dw_b, cv_ln_g, cv_ln_b, cv_pw2_w, cv_pw2_b, da_wqkv, da_wo, da_qn_g, da_kn_g, da_lq1, da_lk1, da_lq2, da_lk2, da_subln_g, ffn_w_up, ffn_dw_w, ffn_dw_b, ffn_w_down):
    bsz, n_lat, d = x.shape
    n_ctx = ctx.shape[1]
    row = lambda a: a.reshape(1, -1)

    c_all = jnp.zeros((MOD_ROWS, d), F32).at[:bsz].set(c).at[MOD_ROWS - 8].set(c_ctx)
    mods = _mods(c_all, ada_w, ada_b)
    tables = _rope_tables(n_lat)

    xl = x.reshape(bsz * n_lat, d)
    xc = ctx.reshape(bsz * n_ctx, d)
    lat = dict(tokens_per_batch=n_lat)
    cx = dict(tokens_per_batch=None)

    for i in range(DEPTH):
        last = i == DEPTH - 1
        j = i // N_MIXERS
        mods_i = mods[i].reshape(MOD_ROWS, 1, 6 * d)
        ng = row(mix_norm_g[i])
        if i % N_MIXERS == 0:
            cp = (ng, cv_pw1_w[j].astype(BF16), row(cv_pw1_b[j]), cv_dw_w[j], row(cv_dw_b[j]),
                  row(cv_ln_g[j]), row(cv_ln_b[j]), cv_pw2_w[j].astype(BF16), row(cv_pw2_b[j]))
            xl = _conv_mixer(xl, mods_i, *cp, row_len=GRID_W, tm=512, **lat)
            if not last:
                xc = _conv_mixer(xc, mods_i, *cp, row_len=n_ctx, tm=n_ctx, **cx)
        else:
            lam_init = 0.8 - 0.6 * math.exp(-0.3 * i)
            w = da_wqkv[j].astype(BF16)
            wo = da_wo[j].astype(BF16)
            qg = row(jnp.tile(da_qn_g[j], 2))
            kg = row(jnp.tile(da_kn_g[j], 2))
            lams = (row(da_lq1[j]), row(da_lk1[j]), row(da_lq2[j]), row(da_lk2[j]), row(da_subln_g[j]))
            q_l, k_l, v_l = _qkv(xl, mods_i, ng, w, qg, kg, tables, tm=512, **lat)
            q_c, k_c, v_c = _qkv(xc, mods_i, ng, w, qg, kg, None, tm=512, **cx)
            sh = lambda a, n: a.reshape(bsz, n, d)
            k_all = jnp.concatenate([sh(k_c, n_ctx), sh(k_l, n_lat)], axis=1)
            v_all = jnp.concatenate([sh(v_c, n_ctx), sh(v_l, n_lat)], axis=1)
            o_l = _attention(sh(q_l, n_lat), k_all, v_all, *lams, lam_init=lam_init, tq=256)
            xl = _out_proj(xl, o_l.reshape(bsz * n_lat, d), mods_i, wo, tm=512, **lat)
            if not last:
                o_c = _attention(sh(q_c, n_ctx), sh(k_c, n_ctx), sh(v_c, n_ctx), *lams,
                                 lam_init=lam_init, tq=n_ctx)
                xc = _out_proj(xc, o_c.reshape(bsz * n_ctx, d), mods_i, wo, tm=512, **cx)

        fp = (row(ffn_norm_g[i]), ffn_w_up[i].astype(BF16), ffn_dw_w[i], row(ffn_dw_b[i]),
              ffn_w_down[i].astype(BF16))
        xl = _ffn(xl, mods_i, *fp, row_len=GRID_W, tm=1024, tf=256, **lat)
        if not last:
            xc = _ffn(xc, mods_i, *fp, row_len=n_ctx, tm=1024, tf=256, **cx)
    return xl.reshape(bsz, n_lat, d)
```

```python
import functools
import math

import jax
import jax.numpy as jnp
from jax import lax
from jax.experimental import pallas as pl
from jax.experimental.pallas import tpu as pltpu

F32 = jnp.float32
BF16 = jnp.bfloat16

D_MODEL = 1024
DEPTH = 4
GRID_W = 64
N_MIXERS = 2
CONV_WIDTH = 31
CONV_HALO = 16
CONV_SEG = 64
DA_HEAD_DIM = 64
DA_V_DIM = 2 * DA_HEAD_DIM
DA_HEADS = D_MODEL // DA_V_DIM
DA_QK_WIDTH = DA_HEADS * 2 * DA_HEAD_DIM
DA_SCALE = DA_HEAD_DIM ** -0.5
ROPE_THETA = 10000.0
D_FF = ((8 * D_MODEL // 3 + 127) // 128) * 128
EPS = 1e-6

LANES = 128
SUBLANES = 8
BF16_ROWS = 16
MOD_ROWS = 40
CTX_MOD_ROW = MOD_ROWS - 8
VMEM_LIMIT = 56 * 1024 * 1024


def _params(sem):
    return pltpu.CompilerParams(dimension_semantics=sem, vmem_limit_bytes=VMEM_LIMIT)


def _silu(x):
    return x * jax.nn.sigmoid(x)


def _mod_rms(x, g, shift, scale):
    ms = jnp.mean(x * x, axis=-1, keepdims=True)
    return (x * lax.rsqrt(ms + EPS) * g) * (1.0 + scale) + shift


def _mods_kernel(c_ref, w_ref, b_ref, o_ref):
    s = _silu(c_ref[...])
    o_ref[0] = jnp.dot(s, w_ref[0], preferred_element_type=F32,
                       precision=lax.Precision.HIGHEST) + b_ref[0]


def _mods(c_all, ada_w, ada_b):
    tn = 512
    n6 = ada_w.shape[-1]
    return pl.pallas_call(
        _mods_kernel,
        grid=(DEPTH, n6 // tn),
        in_specs=[pl.BlockSpec((MOD_ROWS, D_MODEL), lambda i, n: (0, 0)),
                  pl.BlockSpec((1, D_MODEL, tn), lambda i, n: (i, 0, n)),
                  pl.BlockSpec((1, 1, tn), lambda i, n: (i, 0, n))],
        out_specs=pl.BlockSpec((1, MOD_ROWS, tn), lambda i, n: (i, 0, n)),
        out_shape=jax.ShapeDtypeStruct((DEPTH, MOD_ROWS, n6), F32),
        compiler_params=_params(("arbitrary", "arbitrary")),
        name="adaln_maps",
    )(c_all, ada_w, ada_b.reshape(DEPTH, 1, n6))


def _mod_spec(tokens_per_batch, tm):
    if tokens_per_batch is None:
        return pl.BlockSpec((1, 1, 6 * D_MODEL), lambda t: (CTX_MOD_ROW, 0, 0))
    tiles_per_batch = tokens_per_batch // tm
    return pl.BlockSpec((1, 1, 6 * D_MODEL), lambda t: (t // tiles_per_batch, 0, 0))


def _const(shape, n_grid_axes=1):
    zero = (0,) * len(shape)
    index = (lambda t: zero) if n_grid_axes == 1 else (lambda b, t: zero)
    return pl.BlockSpec(shape, index, pipeline_mode=pl.Buffered(1))


def _conv_mixer_kernel(x_ref, mod_ref, ng_ref, w1_ref, b1_ref, dww_ref, dwb_ref, lng_ref, lnb_ref,
                       w2_ref, b2_ref, o_ref, u_ref, sh_ref, y_ref, *, row_len, n_rows):
    d = D_MODEL
    x = x_ref[...]
    mod = mod_ref[0]
    shift, scale, gate = mod[:, 0:d], mod[:, d:2 * d], mod[:, 2 * d:3 * d]
    h = _mod_rms(x, ng_ref[...], shift, scale).astype(BF16)
    u = jnp.dot(h, w1_ref[...], preferred_element_type=F32) + b1_ref[...]
    u_ref[...] = u[:, :d] * jax.nn.sigmoid(u[:, d:])

    zeros = jnp.zeros((SUBLANES, CONV_HALO, d), F32)
    sh_ref[:, 0:CONV_HALO, :] = zeros
    sh_ref[:, row_len + CONV_HALO - SUBLANES:row_len + CONV_HALO + SUBLANES, :] = zeros

    def row_body(r, carry):
        base = pl.multiple_of(r * row_len, CONV_SEG)
        u_row = u_ref[pl.ds(base, row_len), :]
        for j in range(SUBLANES):
            sh_ref[j, CONV_HALO - j:CONV_HALO - j + row_len, :] = u_row
        for p in range(row_len // CONV_SEG):
            for c in range(d // LANES):
                cs = slice(c * LANES, (c + 1) * LANES)
                acc = jnp.broadcast_to(dwb_ref[:, cs], (CONV_SEG, LANES))
                for k in range(CONV_WIDTH):
                    j, a = (k + 1) % SUBLANES, (k + 1) // SUBLANES
                    off = p * CONV_SEG + a * SUBLANES
                    acc = acc + dww_ref[k:k + 1, cs] * sh_ref[j, off:off + CONV_SEG, cs]
                y_ref[pl.ds(base + p * CONV_SEG, CONV_SEG), cs] = acc
        return carry

    lax.fori_loop(0, n_rows, row_body, 0)

    y = y_ref[...]
    mu = jnp.mean(y, axis=-1, keepdims=True)
    yc = y - mu
    var = jnp.mean(yc * yc, axis=-1, keepdims=True)
    z = _silu(yc * lax.rsqrt(var + EPS) * lng_ref[...] + lnb_ref[...])
    out = jnp.dot(z.astype(BF16), w2_ref[...], preferred_element_type=F32) + b2_ref[...]
    o_ref[...] = x + gate * out


def _conv_mixer(x2, mods_i, ng, w1, b1, dww, dwb, lng, lnb, w2, b2, *, tokens_per_batch, row_len, tm):
    n, d = x2.shape
    n_rows = tm // row_len
    kern = functools.partial(_conv_mixer_kernel, row_len=row_len, n_rows=n_rows)
    tile = pl.BlockSpec((tm, d), lambda t: (t, 0))
    return pl.pallas_call(
        kern,
        grid=(n // tm,),
        in_specs=[tile, _mod_spec(tokens_per_batch, tm), _const((1, d)),
                  _const((d, 2 * d)), _const((1, 2 * d)), _const((CONV_WIDTH, d)), _const((1, d)),
                  _const((1, d)), _const((1, d)), _const((d, d)), _const((1, d))],
        out_specs=tile,
        out_shape=jax.ShapeDtypeStruct((n, d), F32),
        scratch_shapes=[pltpu.VMEM((tm, d), F32),
                        pltpu.VMEM((SUBLANES, row_len + 2 * CONV_HALO, d), F32),
                        pltpu.VMEM((tm, d), F32)],
        compiler_params=_params(("arbitrary",)),
        name="conv_mixer",
    )(x2, mods_i, ng, w1, b1, dww, dwb, lng, lnb, w2, b2)


def _head_rms(p, g):
    lane = lax.broadcasted_iota(jnp.int32, p.shape, 1)
    lo = lane < DA_HEAD_DIM
    sq = p * p
    s_lo = jnp.sum(jnp.where(lo, sq, 0.0), axis=-1, keepdims=True)
    s_hi = jnp.sum(jnp.where(lo, 0.0, sq), axis=-1, keepdims=True)
    ms = jnp.where(lo, s_lo, s_hi) * (1.0 / DA_HEAD_DIM)
    return p * lax.rsqrt(ms + EPS) * g


def _rope(x, cos, sin_dn, sin_up):
    quarter = DA_HEAD_DIM // 4
    return (x * cos + pltpu.roll(x, LANES - quarter, axis=1) * sin_dn
            + pltpu.roll(x, quarter, axis=1) * sin_up)


def _qkv_kernel(*refs, rope):
    x_ref, mod_ref, ng_ref, w_ref, qg_ref, kg_ref = refs[:6]
    tabs = refs[6:9] if rope else ()
    q_ref, k_ref, v_ref = refs[-3:]
    d = D_MODEL
    mod = mod_ref[0]
    h = _mod_rms(x_ref[...], ng_ref[...], mod[:, 0:d], mod[:, d:2 * d]).astype(BF16)
    p = jnp.dot(h, w_ref[...], preferred_element_type=F32)
    for hd in range(DA_HEADS):
        cs = slice(hd * DA_V_DIM, (hd + 1) * DA_V_DIM)
        q = _head_rms(p[:, cs], qg_ref[...])
        k = _head_rms(p[:, DA_QK_WIDTH + hd * DA_V_DIM:DA_QK_WIDTH + (hd + 1) * DA_V_DIM], kg_ref[...])
        if rope:
            q = _rope(q, *(t[...] for t in tabs))
            k = _rope(k, *(t[...] for t in tabs))
        q_ref[0, :, cs] = (q * (DA_SCALE * math.log2(math.e))).astype(BF16)
        k_ref[0, :, cs] = k.astype(BF16)
    v_ref[0] = p[:, 2 * DA_QK_WIDTH:].astype(BF16)


def _qkv(x2, mods_i, ng, w, qg, kg, tables, kv_bufs, *, bsz, is_ctx, tm, kv_len, kv_off):
    n, d = x2.shape
    n_tok = n // bsz
    tpb = n_tok // tm
    rope = tables is not None
    tile = pl.BlockSpec((tm, d), lambda t: (t, 0))
    in_specs = [tile, _mod_spec(None if is_ctx else n_tok, tm), _const((1, d)), _const((d, 3 * d)),
                _const((1, DA_V_DIM)), _const((1, DA_V_DIM))]
    args = [x2, mods_i, ng, w, qg, kg]
    if rope:
        tab = pl.BlockSpec((tm, LANES), lambda t: (t % tpb, 0))
        in_specs += [tab, tab, tab]
        args += list(tables)
    aliases = {}
    if kv_bufs is not None:
        aliases = {len(args): 1, len(args) + 1: 2}
        in_specs += [pl.BlockSpec(memory_space=pl.ANY)] * 2
        args += list(kv_bufs)
    q_spec = pl.BlockSpec((1, tm, d), lambda t: (t // tpb, t % tpb, 0))
    kv_spec = pl.BlockSpec((1, tm, d), lambda t: (t // tpb, kv_off // tm + t % tpb, 0))
    kv_shape = jax.ShapeDtypeStruct((bsz, kv_len, d), BF16)
    return pl.pallas_call(
        functools.partial(_qkv_kernel, rope=rope),
        grid=(n // tm,),
        in_specs=in_specs,
        out_specs=[q_spec, kv_spec, kv_spec],
        out_shape=[jax.ShapeDtypeStruct((bsz, n_tok, d), BF16), kv_shape, kv_shape],
        input_output_aliases=aliases,
        compiler_params=_params(("arbitrary",)),
        name="qkv_rope" if rope else "qkv",
    )(*args)


def _attn_kernel(q_ref, k_ref, v_ref, x_ref, mod_ref, lq1_ref, lk1_ref, lq2_ref, lk2_ref, sg_ref, wo_ref,
                 o_ref, s_ref, a_ref, oh_ref, *, lam_init):
    d = D_MODEL
    tq = q_ref.shape[1]
    lam = (jnp.exp(jnp.sum(lq1_ref[...] * lk1_ref[...], axis=-1, keepdims=True))
           - jnp.exp(jnp.sum(lq2_ref[...] * lk2_ref[...], axis=-1, keepdims=True)) + lam_init)
    lk = k_ref.shape[1]
    lo = lax.broadcasted_iota(jnp.int32, (tq, DA_V_DIM), 1) < DA_HEAD_DIM
    ones = jnp.ones((lk, DA_V_DIM), BF16)
    nt = (((1,), (1,)), ((), ()))
    for hd in range(DA_HEADS):
        cs = slice(hd * DA_V_DIM, (hd + 1) * DA_V_DIM)
        buf = hd % 2
        q = q_ref[0, :, cs]
        k = k_ref[0, :, cs]
        zero = jnp.zeros_like(q)
        v_ext = jnp.concatenate([v_ref[0, :, cs], ones], axis=-1)
        heads = []
        for comp in range(2):
            q_c = jnp.where(lo, q, zero) if comp == 0 else jnp.where(lo, zero, q)
            s = lax.dot_general(q_c, k, nt, preferred_element_type=F32)
            s_ref[buf, comp] = s
            m = jnp.max(s, axis=-1, keepdims=True)
            a_ref[buf, comp] = jnp.exp2(s_ref[buf, comp] - m).astype(BF16)
            ol = jnp.dot(a_ref[buf, comp], v_ext, preferred_element_type=F32)
            heads.append(ol[:, :DA_V_DIM] / ol[:, DA_V_DIM:])
        o = heads[0] - lam * heads[1]
        ms = jnp.mean(o * o, axis=-1, keepdims=True)
        oh_ref[:, cs] = (o * lax.rsqrt(ms + EPS) * sg_ref[...] * (1.0 - lam_init)).astype(BF16)
    gate = mod_ref[0][:, 2 * d:3 * d]
    o_ref[...] = x_ref[...] + gate * jnp.dot(oh_ref[...], wo_ref[...], preferred_element_type=F32)


def _attention(q, k_all, v_all, x2, mods_i, lq1, lk1, lq2, lk2, sg, wo, *, lam_init, is_ctx, tq, kv_lo, kv_n):
    b, lq, d = q.shape
    nq = lq // tq
    vec = _const((1, DA_HEAD_DIM), 2)
    kv_spec = pl.BlockSpec((1, kv_n, d), lambda bi, t: (bi, kv_lo // kv_n, 0))
    tile = pl.BlockSpec((tq, d), lambda bi, t: (bi * nq + t, 0))
    mod_row = (lambda bi, t: (CTX_MOD_ROW, 0, 0)) if is_ctx else (lambda bi, t: (bi, 0, 0))
    return pl.pallas_call(
        functools.partial(_attn_kernel, lam_init=lam_init),
        grid=(b, nq),
        in_specs=[pl.BlockSpec((1, tq, d), lambda bi, t: (bi, t, 0)), kv_spec, kv_spec, tile,
                  pl.BlockSpec((1, 1, 6 * d), mod_row),
                  vec, vec, vec, vec, _const((1, DA_V_DIM), 2), _const((d, d), 2)],
        out_specs=tile,
        out_shape=jax.ShapeDtypeStruct(x2.shape, F32),
        scratch_shapes=[pltpu.VMEM((2, 2, tq, kv_n), F32), pltpu.VMEM((2, 2, tq, kv_n), BF16),
                        pltpu.VMEM((tq, d), BF16)],
        compiler_params=_params(("arbitrary", "arbitrary")),
        name="diff_attention",
    )(q, k_all, v_all, x2, mods_i, lq1, lk1, lq2, lk2, sg, wo)


def _dwconv3(u, w, b, edge, row_len):
    tm, tf = u.shape
    first, last = edge
    shape = (tm // row_len, row_len, tf)
    w_prev = jnp.where(first, 0.0, w[0:1, :])
    w_next = jnp.where(last, 0.0, w[2:3, :])
    prev = pltpu.roll(u, 1, axis=0).reshape(shape)
    nxt = pltpu.roll(u, tm - 1, axis=0).reshape(shape)
    return prev * w_prev + u.reshape(shape) * w[1:2, :] + nxt * w_next + b


def _ffn_kernel(x_ref, mod_ref, ng_ref, wup_ref, dww_ref, dwb_ref, wdn_ref, o_ref, f_ref, hid_ref,
                *, row_len, n_chunks):
    d = D_MODEL
    tm = x_ref.shape[0]
    tf = wup_ref.shape[-1]
    mod = mod_ref[0]
    f_ref[...] = _mod_rms(x_ref[...], ng_ref[...], mod[:, 3 * d:4 * d], mod[:, 4 * d:5 * d]).astype(BF16)
    t = lax.broadcasted_iota(jnp.int32, (row_len, 1), 0)
    edge = (t == 0, t == row_len - 1)

    def branch(idx):
        u = jnp.dot(f_ref[...], wup_ref[idx], preferred_element_type=F32)
        return _dwconv3(u, dww_ref[idx], dwb_ref[idx], edge, row_len)

    for j in range(n_chunks):
        hid = _silu(branch(j)) * branch(n_chunks + j)
        hid_ref[:, j * tf:(j + 1) * tf] = hid.reshape(tm, tf).astype(BF16)

    out = jnp.dot(hid_ref[...], wdn_ref[...], preferred_element_type=F32)
    o_ref[...] = x_ref[...] + mod[:, 5 * d:6 * d] * out


def _ffn(x2, mods_i, ng, w_up, dw_w, dw_b, w_down, *, tokens_per_batch, row_len, tm, tf):
    n, d = x2.shape
    nj = D_FF // tf
    tile = pl.BlockSpec((tm, d), lambda t: (t, 0))
    w_up3 = w_up.reshape(d, 2 * nj, tf).transpose(1, 0, 2)
    dw_w3 = dw_w.reshape(3, 2 * nj, tf).transpose(1, 0, 2)
    dw_b3 = dw_b.reshape(2 * nj, 1, tf)
    return pl.pallas_call(
        functools.partial(_ffn_kernel, row_len=row_len, n_chunks=nj),
        grid=(n // tm,),
        in_specs=[tile, _mod_spec(tokens_per_batch, tm), _const((1, d)),
                  _const((2 * nj, d, tf)), _const((2 * nj, 3, tf)), _const((2 * nj, 1, tf)),
                  _const((D_FF, d))],
        out_specs=tile,
        out_shape=jax.ShapeDtypeStruct((n, d), F32),
        scratch_shapes=[pltpu.VMEM((tm, d), BF16), pltpu.VMEM((tm, D_FF), BF16)],
        compiler_params=_params(("arbitrary",)),
        name="conv_ffn",
    )(x2, mods_i, ng, w_up3, dw_w3, dw_b3, w_down)


def _rope_tables(n_tok):
    t = jnp.arange(n_tok)
    row = (t // GRID_W).astype(F32)
    col = (t % GRID_W).astype(F32)
    half = DA_HEAD_DIM // 2
    inv = ROPE_THETA ** (-jnp.arange(0, half, 2, dtype=F32) / half)
    ang_r = row[:, None] * inv
    ang_c = col[:, None] * inv
    ang = jnp.concatenate([ang_r, ang_r, ang_c, ang_c], axis=-1)
    ang = jnp.concatenate([ang, ang], axis=-1)
    first = (jnp.arange(LANES) % (2 * (DA_HEAD_DIM // 4))) < DA_HEAD_DIM // 4
    sin = jnp.sin(ang)
    return jnp.cos(ang), jnp.where(first, -sin, 0.0), jnp.where(first, 0.0, sin)


def kernel(x, c, ctx, c_ctx, ada_w, ada_b, mix_norm_g, ffn_norm_g, cv_pw1_w, cv_pw1_b, cv_dw_w, cv_dw_b, cv_ln_g, cv_ln_b, cv_pw2_w, cv_pw2_b, da_wqkv, da_wo, da_qn_g, da_kn_g, da_lq1, da_lk1, da_lq2, da_lk2, da_subln_g, ffn_w_up, ffn_dw_w, ffn_dw_b, ffn_w_down):
    bsz, n_lat, d = x.shape
    n_ctx = ctx.shape[1]
    row = lambda a: a.reshape(1, -1)

    c_all = jnp.zeros((MOD_ROWS, d), F32).at[:bsz].set(c).at[CTX_MOD_ROW].set(c_ctx)
    mods = _mods(c_all, ada_w, ada_b)
    tables = _rope_tables(n_lat)

    xl = x.reshape(bsz * n_lat, d)
    xc = ctx.reshape(bsz * n_ctx, d)
    lat = dict(tokens_per_batch=n_lat)
    cx = dict(tokens_per_batch=None)

    for i in range(DEPTH):
        last = i == DEPTH - 1
        j = i // N_MIXERS
        mods_i = mods[i].reshape(MOD_ROWS, 1, 6 * d)
        ng = row(mix_norm_g[i])
        if i % N_MIXERS == 0:
            cp = (ng, cv_pw1_w[j].astype(BF16), row(cv_pw1_b[j]), cv_dw_w[j], row(cv_dw_b[j]),
                  row(cv_ln_g[j]), row(cv_ln_b[j]), cv_pw2_w[j].astype(BF16), row(cv_pw2_b[j]))
            xl = _conv_mixer(xl, mods_i, *cp, row_len=GRID_W, tm=512, **lat)
            if not last:
                xc = _conv_mixer(xc, mods_i, *cp, row_len=n_ctx, tm=n_ctx, **cx)
        else:
            lam_init = 0.8 - 0.6 * math.exp(-0.3 * i)
            w = da_wqkv[j].astype(BF16)
            qg = row(jnp.tile(da_qn_g[j], 2))
            kg = row(jnp.tile(da_kn_g[j], 2))
            ap = (mods_i, row(da_lq1[j]), row(da_lk1[j]), row(da_lq2[j]), row(da_lk2[j]),
                  row(da_subln_g[j]), da_wo[j].astype(BF16))
            kv = dict(bsz=bsz, kv_len=n_lat + n_ctx)
            q_l, k_all, v_all = _qkv(xl, mods_i, ng, w, qg, kg, tables, None,
                                     is_ctx=False, tm=512, kv_off=0, **kv)
            q_c, k_all, v_all = _qkv(xc, mods_i, ng, w, qg, kg, None, (k_all, v_all),
                                     is_ctx=True, tm=n_ctx, kv_off=n_lat, **kv)
            xl = _attention(q_l, k_all, v_all, xl, *ap, lam_init=lam_init, is_ctx=False, tq=256,
                            kv_lo=0, kv_n=n_lat + n_ctx)
            if not last:
                xc = _attention(q_c, k_all, v_all, xc, *ap, lam_init=lam_init, is_ctx=True, tq=n_ctx,
                                kv_lo=n_lat, kv_n=n_ctx)

        fp =(row(ffn_norm_g[i]), ffn_w_up[i].astype(BF16), ffn_dw_w[i], row(ffn_dw_b[i]),
              ffn_w_down[i].astype(BF16))
        xl = _ffn(xl, mods_i, *fp, row_len=GRID_W, tm=1024, tf=256, **lat)
        if not last:
            xc = _ffn(xc, mods_i, *fp, row_len=n_ctx, tm=1024, tf=256, **cx)
    return xl.reshape(bsz, n_lat, d)
```

```python
import functools
import math

import jax
import jax.numpy as jnp
from jax import lax
from jax.experimental import pallas as pl
from jax.experimental.pallas import tpu as pltpu

F32 = jnp.float32
BF16 = jnp.bfloat16

D_MODEL = 1024
DEPTH = 4
GRID_W = 64
N_MIXERS = 2
CONV_WIDTH = 31
CONV_HALO = 16
CONV_SEG = 64
DA_HEAD_DIM = 64
DA_V_DIM = 2 * DA_HEAD_DIM
DA_HEADS = D_MODEL // DA_V_DIM
DA_QK_WIDTH = DA_HEADS * 2 * DA_HEAD_DIM
DA_SCALE = DA_HEAD_DIM ** -0.5
ROPE_THETA = 10000.0
D_FF = ((8 * D_MODEL // 3 + 127) // 128) * 128
EPS = 1e-6

LANES = 128
SUBLANES = 8
MXU_TILE = 256
COL_SLAB = 2 * MXU_TILE
FFN_SUB_TILES = 1
ATTN_SUB_TILES = 1
MOD_ROWS = 40
CTX_MOD_ROW = MOD_ROWS - 8
VMEM_LIMIT = 56 * 1024 * 1024


def _params(sem):
    return pltpu.CompilerParams(dimension_semantics=sem, vmem_limit_bytes=VMEM_LIMIT)


def _silu(x):
    return x * jax.nn.sigmoid(x)


def _mod_rms(x, g, shift, scale):
    ms = jnp.mean(x * x, axis=-1, keepdims=True)
    return (x * lax.rsqrt(ms + EPS) * g) * (1.0 + scale) + shift


def _mods_kernel(c_ref, w_ref, b_ref, o_ref):
    s = _silu(c_ref[...])
    o_ref[0] = jnp.dot(s, w_ref[0], preferred_element_type=F32,
                       precision=lax.Precision.HIGHEST) + b_ref[0]


def _mods(c_all, ada_w, ada_b):
    tn = 512
    n6 = ada_w.shape[-1]
    return pl.pallas_call(
        _mods_kernel,
        grid=(DEPTH, n6 // tn),
        in_specs=[pl.BlockSpec((MOD_ROWS, D_MODEL), lambda i, n: (0, 0)),
                  pl.BlockSpec((1, D_MODEL, tn), lambda i, n: (i, 0, n)),
                  pl.BlockSpec((1, 1, tn), lambda i, n: (i, 0, n))],
        out_specs=pl.BlockSpec((1, MOD_ROWS, tn), lambda i, n: (i, 0, n)),
        out_shape=jax.ShapeDtypeStruct((DEPTH, MOD_ROWS, n6), F32),
        compiler_params=_params(("arbitrary", "arbitrary")),
        name="adaln_maps",
    )(c_all, ada_w, ada_b.reshape(DEPTH, 1, n6))


def _mod_spec(tokens_per_batch, tm):
    if tokens_per_batch is None:
        return pl.BlockSpec((1, 1, 6 * D_MODEL), lambda t: (CTX_MOD_ROW, 0, 0))
    tiles_per_batch = tokens_per_batch // tm
    return pl.BlockSpec((1, 1, 6 * D_MODEL), lambda t: (t // tiles_per_batch, 0, 0))


def _const(shape, n_grid_axes=1):
    zero = (0,) * len(shape)
    index = (lambda t: zero) if n_grid_axes == 1 else (lambda b, t: zero)
    return pl.BlockSpec(shape, index, pipeline_mode=pl.Buffered(1))


def _conv_mixer_kernel(x_ref, mod_ref, ng_ref, w1_ref, b1_ref, dww_ref, dwb_ref, lng_ref, lnb_ref,
                       w2_ref, b2_ref, o_ref, u_ref, sh_ref, y_ref, *, row_len, n_rows):
    d = D_MODEL
    x = x_ref[...]
    mod = mod_ref[0]
    shift, scale, gate = mod[:, 0:d], mod[:, d:2 * d], mod[:, 2 * d:3 * d]
    h = _mod_rms(x, ng_ref[...], shift, scale).astype(BF16)
    u = jnp.dot(h, w1_ref[...], preferred_element_type=F32) + b1_ref[...]
    u_ref[...] = u[:, :d] * jax.nn.sigmoid(u[:, d:])

    zeros = jnp.zeros((SUBLANES, CONV_HALO, d), F32)
    sh_ref[:, 0:CONV_HALO, :] = zeros
    sh_ref[:, row_len + CONV_HALO - SUBLANES:row_len + CONV_HALO + SUBLANES, :] = zeros

    def row_body(r, carry):
        base = pl.multiple_of(r * row_len, CONV_SEG)
        u_row = u_ref[pl.ds(base, row_len), :]
        for j in range(SUBLANES):
            sh_ref[j, CONV_HALO - j:CONV_HALO - j + row_len, :] = u_row
        for p in range(row_len // CONV_SEG):
            for c in range(d // LANES):
                cs = slice(c * LANES, (c + 1) * LANES)
                acc = jnp.broadcast_to(dwb_ref[:, cs], (CONV_SEG, LANES))
                for k in range(CONV_WIDTH):
                    j, a = (k + 1) % SUBLANES, (k + 1) // SUBLANES
                    off = p * CONV_SEG + a * SUBLANES
                    acc = acc + dww_ref[k:k + 1, cs] * sh_ref[j, off:off + CONV_SEG, cs]
                y_ref[pl.ds(base + p * CONV_SEG, CONV_SEG), cs] = acc
        return carry

    lax.fori_loop(0, n_rows, row_body, 0)

    y = y_ref[...]
    mu = jnp.mean(y, axis=-1, keepdims=True)
    yc = y - mu
    var = jnp.mean(yc * yc, axis=-1, keepdims=True)
    z = _silu(yc * lax.rsqrt(var + EPS) * lng_ref[...] + lnb_ref[...])
    out = jnp.dot(z.astype(BF16), w2_ref[...], preferred_element_type=F32) + b2_ref[...]
    o_ref[...] = x + gate * out


def _conv_mixer(x2, mods_i, ng, w1, b1, dww, dwb, lng, lnb, w2, b2, *, tokens_per_batch, row_len, tm):
    n, d = x2.shape
    n_rows = tm // row_len
    kern = functools.partial(_conv_mixer_kernel, row_len=row_len, n_rows=n_rows)
    tile = pl.BlockSpec((tm, d), lambda t: (t, 0))
    return pl.pallas_call(
        kern,
        grid=(n // tm,),
        in_specs=[tile, _mod_spec(tokens_per_batch, tm), _const((1, d)),
                  _const((d, 2 * d)), _const((1, 2 * d)), _const((CONV_WIDTH, d)), _const((1, d)),
                  _const((1, d)), _const((1, d)), _const((d, d)), _const((1, d))],
        out_specs=tile,
        out_shape=jax.ShapeDtypeStruct((n, d), F32),
        scratch_shapes=[pltpu.VMEM((tm, d), F32),
                        pltpu.VMEM((SUBLANES, row_len + 2 * CONV_HALO, d), F32),
                        pltpu.VMEM((tm, d), F32)],
        compiler_params=_params(("arbitrary",)),
        name="conv_mixer",
    )(x2, mods_i, ng, w1, b1, dww, dwb, lng, lnb, w2, b2)


def _group_mean_matrix():
    g = jnp.arange(MXU_TILE) // DA_HEAD_DIM
    return jnp.where(g[:, None] == g[None, :], 1.0 / DA_HEAD_DIM, 0.0).astype(BF16)


def _rope(x, cos, sin_dn, sin_up):
    quarter = DA_HEAD_DIM // 4
    return (x * cos + pltpu.roll(x, LANES - quarter, axis=1) * sin_dn
            + pltpu.roll(x, quarter, axis=1) * sin_up)


def _qkv_kernel(*refs, rope):
    x_ref, mod_ref, ng_ref, w_ref, qg_ref, kg_ref, gm_ref = refs[:7]
    tabs = refs[7:10] if rope else ()
    q_ref, k_ref, v_ref, h_ref = refs[-4:]
    d = D_MODEL
    mod = mod_ref[0]
    h_ref[...] = _mod_rms(x_ref[...], ng_ref[...], mod[:, 0:d], mod[:, d:2 * d]).astype(BF16)
    v_ref[...] = jnp.dot(h_ref[...], w_ref[:, 2 * DA_QK_WIDTH:], preferred_element_type=F32).astype(BF16)
    for base, g_ref, out_ref, scale in ((0, qg_ref, q_ref, DA_SCALE * math.log2(math.e)),
                                        (DA_QK_WIDTH, kg_ref, k_ref, None)):
        for c0 in range(0, DA_QK_WIDTH, COL_SLAB):
            slab = jnp.dot(h_ref[...], w_ref[:, base + c0:base + c0 + COL_SLAB], preferred_element_type=F32)
            for c1 in range(0, COL_SLAB, MXU_TILE):
                u = slab[:, c1:c1 + MXU_TILE]
                ms = jnp.dot((u * u).astype(BF16), gm_ref[...], preferred_element_type=F32)
                u = u * lax.rsqrt(ms + EPS) * g_ref[...]
                for c2 in range(0, MXU_TILE, DA_V_DIM):
                    y = u[:, c2:c2 + DA_V_DIM]
                    if rope:
                        y = _rope(y, *(t[...] for t in tabs))
                    if scale is not None:
                        y = y * scale
                    c = c0 + c1 + c2
                    out_ref[:, c:c + DA_V_DIM] = y.astype(BF16)


def _qkv(x2, mods_i, ng, w, qg, kg, gm, tables, *, tokens_per_batch, tm):
    n, d = x2.shape
    rope = tables is not None
    tile = pl.BlockSpec((tm, d), lambda t: (t, 0))
    in_specs = [tile, _mod_spec(tokens_per_batch, tm), _const((1, d)), _const((d, 3 * d)),
                _const((1, MXU_TILE)), _const((1, MXU_TILE)), _const((MXU_TILE, MXU_TILE))]
    args = [x2, mods_i, ng, w, qg, kg, gm]
    if rope:
        tiles_per_batch = tokens_per_batch // tm
        tab = pl.BlockSpec((tm, LANES), lambda t: (t % tiles_per_batch, 0))
        in_specs += [tab, tab, tab]
        args += list(tables)
    out = jax.ShapeDtypeStruct((n, d), BF16)
    return pl.pallas_call(
        functools.partial(_qkv_kernel, rope=rope),
        grid=(n // tm,),
        in_specs=in_specs,
        out_specs=[tile, tile, tile],
        out_shape=[out, out, out],
        scratch_shapes=[pltpu.VMEM((tm, d), BF16)],
        compiler_params=_params(("arbitrary",)),
        name="qkv_rope" if rope else "qkv",
    )(*args)


def _attn_kernel(*refs, lam_init, n_src, n_sub):
    q_ref = refs[0]
    srcs = [(refs[1 + 2 * i], refs[2 + 2 * i]) for i in range(n_src)]
    (x_ref, mod_ref, lq1_ref, lk1_ref, lq2_ref, lk2_ref, sg_ref, wo_ref,
     o_ref, s_ref, a_ref, oh_ref) = refs[1 + 2 * n_src:]
    d = D_MODEL
    tq = q_ref.shape[1] // n_sub
    lam = (jnp.exp(jnp.sum(lq1_ref[...] * lk1_ref[...], axis=-1, keepdims=True))
           - jnp.exp(jnp.sum(lq2_ref[...] * lk2_ref[...], axis=-1, keepdims=True)) + lam_init)
    lens = [k_ref.shape[1] for k_ref, _ in srcs]
    offs = [sum(lens[:i]) for i in range(n_src)]
    lo = lax.broadcasted_iota(jnp.int32, (tq, DA_V_DIM), 1) < DA_HEAD_DIM
    nt = (((1,), (1,)), ((), ()))
    for sub in range(n_sub):
        rows = slice(sub * tq, (sub + 1) * tq)
        for hd in range(DA_HEADS):
            cs = slice(hd * DA_V_DIM, (hd + 1) * DA_V_DIM)
            buf = hd % 2
            q = q_ref[0, rows, cs]
            zero = jnp.zeros_like(q)
            heads = []
            for comp in range(2):
                q_c = jnp.where(lo, q, zero) if comp == 0 else jnp.where(lo, zero, q)
                m = None
                for (k_ref, _), off, n_k in zip(srcs, offs, lens):
                    s = lax.dot_general(q_c, k_ref[0, :, cs], nt, preferred_element_type=F32)
                    s_ref[buf, comp, :, off:off + n_k] = s
                    m_src = jnp.max(s, axis=-1, keepdims=True)
                    m = m_src if m is None else jnp.maximum(m, m_src)
                a_ref[buf, comp] = jnp.exp2(s_ref[buf, comp] - m).astype(BF16)
                ol = None
                for (_, v_ref), off, n_k in zip(srcs, offs, lens):
                    v_ext = jnp.concatenate([v_ref[0, :, cs], jnp.ones((n_k, DA_V_DIM), BF16)], axis=-1)
                    part = jnp.dot(a_ref[buf, comp, :, off:off + n_k], v_ext, preferred_element_type=F32)
                    ol = part if ol is None else ol + part
                heads.append(ol[:, :DA_V_DIM] / ol[:, DA_V_DIM:])
            o = heads[0] - lam * heads[1]
            ms = jnp.mean(o * o, axis=-1, keepdims=True)
            oh_ref[rows, cs] = (o * lax.rsqrt(ms + EPS) * sg_ref[...] * (1.0 - lam_init)).astype(BF16)
        y = jnp.dot(oh_ref[rows, :], wo_ref[...], preferred_element_type=F32)
        o_ref[rows, :] = x_ref[rows, :] + mod_ref[0][:, 2 * d:3 * d] * y


def _attention(q, kv, x2, mods_i, lq1, lk1, lq2, lk2, sg, wo, *, lam_init, is_ctx, tq, n_sub):
    b, lq, d = q.shape
    nq = lq // tq
    n_keys = sum(k.shape[1] for k, _ in kv)
    vec = _const((1, DA_HEAD_DIM), 2)
    tile = pl.BlockSpec((tq, d), lambda bi, t: (bi * nq + t, 0))
    mod_row = (lambda bi, t: (CTX_MOD_ROW, 0, 0)) if is_ctx else (lambda bi, t: (bi, 0, 0))
    kv_specs, kv_args = [], []
    for k, v in kv:
        spec = pl.BlockSpec((1, k.shape[1], d), lambda bi, t: (bi, 0, 0))
        kv_specs += [spec, spec]
        kv_args += [k, v]
    return pl.pallas_call(
        functools.partial(_attn_kernel, lam_init=lam_init, n_src=len(kv), n_sub=n_sub),
        grid=(b, nq),
        in_specs=[pl.BlockSpec((1, tq, d), lambda bi, t: (bi, t, 0))] + kv_specs + [
            tile, pl.BlockSpec((1, 1, 6 * d), mod_row),
            vec, vec, vec, vec, _const((1, DA_V_DIM), 2), _const((d, d), 2)],
        out_specs=tile,
        out_shape=jax.ShapeDtypeStruct(x2.shape, F32),
        scratch_shapes=[pltpu.VMEM((2, 2, tq // n_sub, n_keys), F32),
                        pltpu.VMEM((2, 2, tq // n_sub, n_keys), BF16),
                        pltpu.VMEM((tq, d), BF16)],
        compiler_params=_params(("arbitrary", "arbitrary")),
        name="diff_attention",
    )(q, *kv_args, x2, mods_i, lq1, lk1, lq2, lk2, sg, wo)


def _dwconv3(u, w, b, edge, row_len):
    tm, tf = u.shape
    first, last = edge
    shape = (tm // row_len, row_len, tf)
    w_prev = jnp.where(first, 0.0, w[0:1, :])
    w_next = jnp.where(last, 0.0, w[2:3, :])
    prev = pltpu.roll(u, 1, axis=0).reshape(shape)
    nxt = pltpu.roll(u, tm - 1, axis=0).reshape(shape)
    return prev * w_prev + u.reshape(shape) * w[1:2, :] + nxt * w_next + b


def _ffn_kernel(x_ref, mod_ref, ng_ref, wup_ref, dww_ref, dwb_ref, wdn_ref, o_ref, f_ref, hid_ref,
                *, row_len, n_chunks, n_sub):
    d = D_MODEL
    tm = x_ref.shape[0] // n_sub
    tf = wup_ref.shape[-1]
    mod = mod_ref[0]
    t = lax.broadcasted_iota(jnp.int32, (row_len, 1), 0)
    edge = (t == 0, t == row_len - 1)

    for sub in range(n_sub):
        rows = slice(sub * tm, (sub + 1) * tm)
        f_ref[rows, :] = _mod_rms(x_ref[rows, :], ng_ref[...], mod[:, 3 * d:4 * d],
                                  mod[:, 4 * d:5 * d]).astype(BF16)

        def branch(idx):
            u = jnp.dot(f_ref[rows, :], wup_ref[idx], preferred_element_type=F32)
            return _dwconv3(u, dww_ref[idx], dwb_ref[idx], edge, row_len)

        for j in range(n_chunks):
            hid = _silu(branch(j)) * branch(n_chunks + j)
            hid_ref[rows, j * tf:(j + 1) * tf] = hid.reshape(tm, tf).astype(BF16)

        out = jnp.dot(hid_ref[rows, :], wdn_ref[...], preferred_element_type=F32)
        o_ref[rows, :] = x_ref[rows, :] + mod[:, 5 * d:6 * d] * out


def _ffn(x2, mods_i, ng, w_up, dw_w, dw_b, w_down, *, tokens_per_batch, row_len, tm, tf):
    n, d = x2.shape
    nj = D_FF // tf
    tile = pl.BlockSpec((tm, d), lambda t: (t, 0))
    w_up3 = w_up.reshape(d, 2 * nj, tf).transpose(1, 0, 2)
    dw_w3 = dw_w.reshape(3, 2 * nj, tf).transpose(1, 0, 2)
    dw_b3 = dw_b.reshape(2 * nj, 1, tf)
    return pl.pallas_call(
        functools.partial(_ffn_kernel, row_len=row_len, n_chunks=nj, n_sub=FFN_SUB_TILES),
        grid=(n // tm,),
        in_specs=[tile, _mod_spec(tokens_per_batch, tm), _const((1, d)),
                  _const((2 * nj, d, tf)), _const((2 * nj, 3, tf)), _const((2 * nj, 1, tf)),
                  _const((D_FF, d))],
        out_specs=tile,
        out_shape=jax.ShapeDtypeStruct((n, d), F32),
        scratch_shapes=[pltpu.VMEM((tm, d), BF16), pltpu.VMEM((tm, D_FF), BF16)],
        compiler_params=_params(("arbitrary",)),
        name="conv_ffn",
    )(x2, mods_i, ng, w_up3, dw_w3, dw_b3, w_down)


def _rope_tables(n_tok):
    t = jnp.arange(n_tok)
    row = (t // GRID_W).astype(F32)
    col = (t % GRID_W).astype(F32)
    half = DA_HEAD_DIM // 2
    inv = ROPE_THETA ** (-jnp.arange(0, half, 2, dtype=F32) / half)
    ang_r = row[:, None] * inv
    ang_c = col[:, None] * inv
    ang = jnp.concatenate([ang_r, ang_r, ang_c, ang_c], axis=-1)
    ang = jnp.concatenate([ang, ang], axis=-1)
    first = (jnp.arange(LANES) % (2 * (DA_HEAD_DIM // 4))) < DA_HEAD_DIM // 4
    sin = jnp.sin(ang)
    return jnp.cos(ang), jnp.where(first, -sin, 0.0), jnp.where(first, 0.0, sin)


def kernel(x, c, ctx, c_ctx, ada_w, ada_b, mix_norm_g, ffn_norm_g, cv_pw1_w, cv_pw1_b, cv_dw_w, cv_dw_b, cv_ln_g, cv_ln_b, cv_pw2_w, cv_pw2_b, da_wqkv, da_wo, da_qn_g, da_kn_g, da_lq1, da_lk1, da_lq2, da_lk2, da_subln_g, ffn_w_up, ffn_dw_w, ffn_dw_b, ffn_w_down):
    bsz, n_lat, d = x.shape
    n_ctx = ctx.shape[1]
    row = lambda a: a.reshape(1, -1)

    c_all = jnp.zeros((MOD_ROWS, d), F32).at[:bsz].set(c).at[CTX_MOD_ROW].set(c_ctx)
    mods = _mods(c_all, ada_w, ada_b)
    tables = _rope_tables(n_lat)
    gm = _group_mean_matrix()

    xl = x.reshape(bsz * n_lat, d)
    xc = ctx.reshape(bsz * n_ctx, d)
    lat = dict(tokens_per_batch=n_lat)
    cx = dict(tokens_per_batch=None)

    for i in range(DEPTH):
        last = i == DEPTH - 1
        j = i // N_MIXERS
        mods_i = mods[i].reshape(MOD_ROWS, 1, 6 * d)
        ng = row(mix_norm_g[i])
        if i % N_MIXERS == 0:
            cp = (ng, cv_pw1_w[j].astype(BF16), row(cv_pw1_b[j]), cv_dw_w[j], row(cv_dw_b[j]),
                  row(cv_ln_g[j]), row(cv_ln_b[j]), cv_pw2_w[j].astype(BF16), row(cv_pw2_b[j]))
            xl = _conv_mixer(xl, mods_i, *cp, row_len=GRID_W, tm=512, **lat)
            if not last:
                xc = _conv_mixer(xc, mods_i, *cp, row_len=n_ctx, tm=n_ctx, **cx)
        else:
            lam_init = 0.8 - 0.6 * math.exp(-0.3 * i)
            w = da_wqkv[j].astype(BF16)
            qg = row(jnp.tile(da_qn_g[j], MXU_TILE // DA_HEAD_DIM))
            kg = row(jnp.tile(da_kn_g[j], MXU_TILE // DA_HEAD_DIM))
            ap = (mods_i, row(da_lq1[j]), row(da_lk1[j]), row(da_lq2[j]), row(da_lk2[j]),
                  row(da_subln_g[j]), da_wo[j].astype(BF16))
            sh = lambda a, n: a.reshape(bsz, n, d)
            q_l, k_l, v_l = _qkv(xl, mods_i, ng, w, qg, kg, gm, tables, tm=512, **lat)
            q_c, k_c, v_c = _qkv(xc, mods_i, ng, w, qg, kg, gm, None, tm=512, **cx)
            kv_l = (sh(k_l, n_lat), sh(v_l, n_lat))
            kv_c = (sh(k_c, n_ctx), sh(v_c, n_ctx))
            xl = _attention(sh(q_l, n_lat), [kv_l, kv_c], xl, *ap, lam_init=lam_init, is_ctx=False,
                            tq=256 * ATTN_SUB_TILES, n_sub=ATTN_SUB_TILES)
            if not last:
                xc = _attention(sh(q_c, n_ctx), [kv_c], xc, *ap, lam_init=lam_init, is_ctx=True,
                                tq=n_ctx, n_sub=1)

        fp = (row(ffn_norm_g[i]), ffn_w_up[i].astype(BF16), ffn_dw_w[i], row(ffn_dw_b[i]),
              ffn_w_down[i].astype(BF16))
        xl = _ffn(xl, mods_i, *fp, row_len=GRID_W, tm=1024, tf=256, **lat)
        if not last:
            xc = _ffn(xc, mods_i, *fp, row_len=n_ctx, tm=1024, tf=256, **cx)
    return xl.reshape(bsz, n_lat, d)
```

```python
import functools
import math

import jax
import jax.numpy as jnp
from jax import lax
from jax.experimental import pallas as pl
from jax.experimental.pallas import tpu as pltpu

F32 = jnp.float32
BF16 = jnp.bfloat16

D_MODEL = 1024
DEPTH = 4
GRID_W = 64
N_MIXERS = 2
CONV_WIDTH = 31
CONV_HALO = 16
CONV_SEG = 64
DA_HEAD_DIM = 64
DA_V_DIM = 2 * DA_HEAD_DIM
DA_HEADS = D_MODEL // DA_V_DIM
DA_QK_WIDTH = DA_HEADS * 2 * DA_HEAD_DIM
DA_SCALE = DA_HEAD_DIM ** -0.5
ROPE_THETA = 10000.0
D_FF = ((8 * D_MODEL // 3 + 127) // 128) * 128
EPS = 1e-6

LANES = 128
SUBLANES = 8
MXU_TILE = 256
COL_SLAB = 2 * MXU_TILE
FFN_SUB_TILES = 1
ATTN_SUB_TILES = 1
MOD_ROWS = 40
CTX_MOD_ROW = MOD_ROWS - 8
VMEM_LIMIT = 56 * 1024 * 1024


def _params(sem):
    return pltpu.CompilerParams(dimension_semantics=sem, vmem_limit_bytes=VMEM_LIMIT)


def _silu(x):
    return x * jax.nn.sigmoid(x)


def _mod_rms(x, g, shift, scale):
    ms = jnp.mean(x * x, axis=-1, keepdims=True)
    return (x * lax.rsqrt(ms + EPS) * g) * (1.0 + scale) + shift


def _mods_kernel(c_ref, w_ref, b_ref, o_ref):
    s = _silu(c_ref[...])
    o_ref[0] = jnp.dot(s, w_ref[0], preferred_element_type=F32,
                       precision=lax.Precision.HIGHEST) + b_ref[0]


def _mods(c_all, ada_w, ada_b):
    tn = 512
    n6 = ada_w.shape[-1]
    return pl.pallas_call(
        _mods_kernel,
        grid=(DEPTH, n6 // tn),
        in_specs=[pl.BlockSpec((MOD_ROWS, D_MODEL), lambda i, n: (0, 0)),
                  pl.BlockSpec((1, D_MODEL, tn), lambda i, n: (i, 0, n)),
                  pl.BlockSpec((1, 1, tn), lambda i, n: (i, 0, n))],
        out_specs=pl.BlockSpec((1, MOD_ROWS, tn), lambda i, n: (i, 0, n)),
        out_shape=jax.ShapeDtypeStruct((DEPTH, MOD_ROWS, n6), F32),
        compiler_params=_params(("arbitrary", "arbitrary")),
        name="adaln_maps",
    )(c_all, ada_w, ada_b.reshape(DEPTH, 1, n6))


def _mod_spec(tokens_per_batch, tm):
    if tokens_per_batch is None:
        return pl.BlockSpec((1, 1, 6 * D_MODEL), lambda t: (CTX_MOD_ROW, 0, 0))
    tiles_per_batch = tokens_per_batch // tm
    return pl.BlockSpec((1, 1, 6 * D_MODEL), lambda t: (t // tiles_per_batch, 0, 0))


def _const(shape, n_grid_axes=1):
    zero = (0,) * len(shape)
    index = (lambda t: zero) if n_grid_axes == 1 else (lambda b, t: zero)
    return pl.BlockSpec(shape, index, pipeline_mode=pl.Buffered(1))


def _conv_mixer_kernel(x_ref, mod_ref, ng_ref, w1_ref, b1_ref, dww_ref, dwb_ref, lng_ref, lnb_ref,
                       w2_ref, b2_ref, o_ref, u_ref, sh_ref, y_ref, *, row_len, n_rows):
    d = D_MODEL
    x = x_ref[...]
    mod = mod_ref[0]
    shift, scale, gate = mod[:, 0:d], mod[:, d:2 * d], mod[:, 2 * d:3 * d]
    h = _mod_rms(x, ng_ref[...], shift, scale).astype(BF16)
    u = jnp.dot(h, w1_ref[...], preferred_element_type=F32) + b1_ref[...]
    u_ref[...] = u[:, :d] * jax.nn.sigmoid(u[:, d:])

    zeros = jnp.zeros((SUBLANES, CONV_HALO, d), F32)
    sh_ref[:, 0:CONV_HALO, :] = zeros
    sh_ref[:, row_len + CONV_HALO - SUBLANES:row_len + CONV_HALO + SUBLANES, :] = zeros

    def row_body(r, carry):
        base = pl.multiple_of(r * row_len, CONV_SEG)
        u_row = u_ref[pl.ds(base, row_len), :]
        for j in range(SUBLANES):
            sh_ref[j, CONV_HALO - j:CONV_HALO - j + row_len, :] = u_row
        for p in range(row_len // CONV_SEG):
            for c in range(d // LANES):
                cs = slice(c * LANES, (c + 1) * LANES)
                acc = jnp.broadcast_to(dwb_ref[:, cs], (CONV_SEG, LANES))
                for k in range(CONV_WIDTH):
                    j, a = (k + 1) % SUBLANES, (k + 1) // SUBLANES
                    off = p * CONV_SEG + a * SUBLANES
                    acc = acc + dww_ref[k:k + 1, cs] * sh_ref[j, off:off + CONV_SEG, cs]
                y_ref[pl.ds(base + p * CONV_SEG, CONV_SEG), cs] = acc
        return carry

    lax.fori_loop(0, n_rows, row_body, 0)

    y = y_ref[...]
    mu = jnp.mean(y, axis=-1, keepdims=True)
    yc = y - mu
    var = jnp.mean(yc * yc, axis=-1, keepdims=True)
    z = _silu(yc * lax.rsqrt(var + EPS) * lng_ref[...] + lnb_ref[...])
    out = jnp.dot(z.astype(BF16), w2_ref[...], preferred_element_type=F32) + b2_ref[...]
    o_ref[...] = x + gate * out


def _conv_mixer(x2, mods_i, ng, w1, b1, dww, dwb, lng, lnb, w2, b2, *, tokens_per_batch, row_len, tm):
    n, d = x2.shape
    n_rows = tm // row_len
    kern = functools.partial(_conv_mixer_kernel, row_len=row_len, n_rows=n_rows)
    tile = pl.BlockSpec((tm, d), lambda t: (t, 0))
    return pl.pallas_call(
        kern,
        grid=(n // tm,),
        in_specs=[tile, _mod_spec(tokens_per_batch, tm), _const((1, d)),
                  _const((d, 2 * d)), _const((1, 2 * d)), _const((CONV_WIDTH, d)), _const((1, d)),
                  _const((1, d)), _const((1, d)), _const((d, d)), _const((1, d))],
        out_specs=tile,
        out_shape=jax.ShapeDtypeStruct((n, d), F32),
        scratch_shapes=[pltpu.VMEM((tm, d), F32),
                        pltpu.VMEM((SUBLANES, row_len + 2 * CONV_HALO, d), F32),
                        pltpu.VMEM((tm, d), F32)],
        compiler_params=_params(("arbitrary",)),
        name="conv_mixer",
    )(x2, mods_i, ng, w1, b1, dww, dwb, lng, lnb, w2, b2)


def _group_mean_matrix():
    g = jnp.arange(MXU_TILE) // DA_HEAD_DIM
    return jnp.where(g[:, None] == g[None, :], 1.0 / DA_HEAD_DIM, 0.0).astype(BF16)


def _rope(x, cos, sin_dn, sin_up):
    quarter = DA_HEAD_DIM // 4
    return (x * cos + pltpu.roll(x, LANES - quarter, axis=1) * sin_dn
            + pltpu.roll(x, quarter, axis=1) * sin_up)


def _qkv_kernel(*refs, rope):
    x_ref, mod_ref, ng_ref, w_ref, qg_ref, kg_ref, gm_ref = refs[:7]
    tabs = refs[7:10] if rope else ()
    q_ref, k_ref, v_ref, h_ref = refs[-4:]
    d = D_MODEL
    mod = mod_ref[0]
    h_ref[...] = _mod_rms(x_ref[...], ng_ref[...], mod[:, 0:d], mod[:, d:2 * d]).astype(BF16)
    v_ref[...] = jnp.dot(h_ref[...], w_ref[:, 2 * DA_QK_WIDTH:], preferred_element_type=F32).astype(BF16)
    for base, g_ref, out_ref, scale in ((0, qg_ref, q_ref, DA_SCALE * math.log2(math.e)),
                                        (DA_QK_WIDTH, kg_ref, k_ref, None)):
        for c0 in range(0, DA_QK_WIDTH, COL_SLAB):
            slab = jnp.dot(h_ref[...], w_ref[:, base + c0:base + c0 + COL_SLAB], preferred_element_type=F32)
            for c1 in range(0, COL_SLAB, MXU_TILE):
                u = slab[:, c1:c1 + MXU_TILE]
                ms = jnp.dot((u * u).astype(BF16), gm_ref[...], preferred_element_type=F32)
                u = u * lax.rsqrt(ms + EPS) * g_ref[...]
                for c2 in range(0, MXU_TILE, DA_V_DIM):
                    y = u[:, c2:c2 + DA_V_DIM]
                    if rope:
                        y = _rope(y, *(t[...] for t in tabs))
                    if scale is not None:
                        y = y * scale
                    c = c0 + c1 + c2
                    out_ref[:, c:c + DA_V_DIM] = y.astype(BF16)


def _qkv(x2, mods_i, ng, w, qg, kg, gm, tables, *, tokens_per_batch, tm):
    n, d = x2.shape
    rope = tables is not None
    tile = pl.BlockSpec((tm, d), lambda t: (t, 0))
    in_specs = [tile, _mod_spec(tokens_per_batch, tm), _const((1, d)), _const((d, 3 * d)),
                _const((1, MXU_TILE)), _const((1, MXU_TILE)), _const((MXU_TILE, MXU_TILE))]
    args = [x2, mods_i, ng, w, qg, kg, gm]
    if rope:
        tiles_per_batch = tokens_per_batch // tm
        tab = pl.BlockSpec((tm, LANES), lambda t: (t % tiles_per_batch, 0))
        in_specs += [tab, tab, tab]
        args += list(tables)
    out = jax.ShapeDtypeStruct((n, d), BF16)
    return pl.pallas_call(
        functools.partial(_qkv_kernel, rope=rope),
        grid=(n // tm,),
        in_specs=in_specs,
        out_specs=[tile, tile, tile],
        out_shape=[out, out, out],
        scratch_shapes=[pltpu.VMEM((tm, d), BF16)],
        compiler_params=_params(("arbitrary",)),
        name="qkv_rope" if rope else "qkv",
    )(*args)


def _attn_kernel(*refs, lam_init, n_src, n_sub):
    q_ref = refs[0]
    srcs = [(refs[1 + 2 * i], refs[2 + 2 * i]) for i in range(n_src)]
    (x_ref, mod_ref, lq1_ref, lk1_ref, lq2_ref, lk2_ref, sg_ref, wo_ref,
     o_ref, s_ref, a_ref, oh_ref) = refs[1 + 2 * n_src:]
    d = D_MODEL
    tq = q_ref.shape[1] // n_sub
    lam = (jnp.exp(jnp.sum(lq1_ref[...] * lk1_ref[...], axis=-1, keepdims=True))
           - jnp.exp(jnp.sum(lq2_ref[...] * lk2_ref[...], axis=-1, keepdims=True)) + lam_init)
    lens = [k_ref.shape[1] for k_ref, _ in srcs]
    offs = [sum(lens[:i]) for i in range(n_src)]
    lo = lax.broadcasted_iota(jnp.int32, (tq, DA_V_DIM), 1) < DA_HEAD_DIM
    nt = (((1,), (1,)), ((), ()))
    for sub in range(n_sub):
        rows = slice(sub * tq, (sub + 1) * tq)
        for hd in range(DA_HEADS):
            cs = slice(hd * DA_V_DIM, (hd + 1) * DA_V_DIM)
            buf = hd % 2
            q = q_ref[0, rows, cs]
            zero = jnp.zeros_like(q)
            heads = []
            for comp in range(2):
                q_c = jnp.where(lo, q, zero) if comp == 0 else jnp.where(lo, zero, q)
                m = None
                for (k_ref, _), off, n_k in zip(srcs, offs, lens):
                    s = lax.dot_general(q_c, k_ref[0, :, cs], nt, preferred_element_type=F32)
                    s_ref[buf, comp, :, off:off + n_k] = s
                    m_src = jnp.max(s, axis=-1, keepdims=True)
                    m = m_src if m is None else jnp.maximum(m, m_src)
                a_ref[comp] = jnp.exp2(s_ref[buf, comp] - m).astype(BF16)
                ol = None
                for (_, v_ref), off, n_k in zip(srcs, offs, lens):
                    v_ext = jnp.concatenate([v_ref[0, :, cs], jnp.ones((n_k, DA_V_DIM), BF16)], axis=-1)
                    part = jnp.dot(a_ref[comp, :, off:off + n_k], v_ext, preferred_element_type=F32)
                    ol = part if ol is None else ol + part
                heads.append(ol[:, :DA_V_DIM] / ol[:, DA_V_DIM:])
            o = heads[0] - lam * heads[1]
            ms = jnp.mean(o * o, axis=-1, keepdims=True)
            oh_ref[rows, cs] = (o * lax.rsqrt(ms + EPS) * sg_ref[...] * (1.0 - lam_init)).astype(BF16)
        y = jnp.dot(oh_ref[rows, :], wo_ref[...], preferred_element_type=F32)
        o_ref[rows, :] = x_ref[rows, :] + mod_ref[0][:, 2 * d:3 * d] * y


def _attention(q, kv, x2, mods_i, lq1, lk1, lq2, lk2, sg, wo, *, lam_init, is_ctx, tq, n_sub):
    b, lq, d = q.shape
    nq = lq // tq
    n_keys = sum(k.shape[1] for k, _ in kv)
    vec = _const((1, DA_HEAD_DIM), 2)
    tile = pl.BlockSpec((tq, d), lambda bi, t: (bi * nq + t, 0))
    mod_row = (lambda bi, t: (CTX_MOD_ROW, 0, 0)) if is_ctx else (lambda bi, t: (bi, 0, 0))
    kv_specs, kv_args = [], []
    for k, v in kv:
        spec = pl.BlockSpec((1, k.shape[1], d), lambda bi, t: (bi, 0, 0), pipeline_mode=pl.Buffered(1))
        kv_specs += [spec, spec]
        kv_args += [k, v]
    return pl.pallas_call(
        functools.partial(_attn_kernel, lam_init=lam_init, n_src=len(kv), n_sub=n_sub),
        grid=(b, nq),
        in_specs=[pl.BlockSpec((1, tq, d), lambda bi, t: (bi, t, 0))] + kv_specs + [
            tile, pl.BlockSpec((1, 1, 6 * d), mod_row),
            vec, vec, vec, vec, _const((1, DA_V_DIM), 2), _const((d, d), 2)],
        out_specs=tile,
        out_shape=jax.ShapeDtypeStruct(x2.shape, F32),
        scratch_shapes=[pltpu.VMEM((2, 2, tq // n_sub, n_keys), F32),
                        pltpu.VMEM((2, tq // n_sub, n_keys), BF16),
                        pltpu.VMEM((tq, d), BF16)],
        compiler_params=_params(("arbitrary", "arbitrary")),
        name="diff_attention",
    )(q, *kv_args, x2, mods_i, lq1, lk1, lq2, lk2, sg, wo)


def _dwconv3(u, w, b, edge, row_len):
    tm, tf = u.shape
    first, last = edge
    shape = (tm // row_len, row_len, tf)
    w_prev = jnp.where(first, 0.0, w[0:1, :])
    w_next = jnp.where(last, 0.0, w[2:3, :])
    prev = pltpu.roll(u, 1, axis=0).reshape(shape)
    nxt = pltpu.roll(u, tm - 1, axis=0).reshape(shape)
    return prev * w_prev + u.reshape(shape) * w[1:2, :] + nxt * w_next + b


def _ffn_kernel(x_ref, mod_ref, ng_ref, wup_ref, dww_ref, dwb_ref, wdn_ref, o_ref, f_ref, hid_ref,
                *, row_len, n_chunks, n_sub):
    d = D_MODEL
    tm = x_ref.shape[0] // n_sub
    tf = wup_ref.shape[-1]
    mod = mod_ref[0]
    t = lax.broadcasted_iota(jnp.int32, (row_len, 1), 0)
    edge = (t == 0, t == row_len - 1)

    for sub in range(n_sub):
        rows = slice(sub * tm, (sub + 1) * tm)
        f_ref[rows, :] = _mod_rms(x_ref[rows, :], ng_ref[...], mod[:, 3 * d:4 * d],
                                  mod[:, 4 * d:5 * d]).astype(BF16)

        def branch(idx):
            u = jnp.dot(f_ref[rows, :], wup_ref[idx], preferred_element_type=F32)
            return _dwconv3(u, dww_ref[idx], dwb_ref[idx], edge, row_len)

        for j in range(n_chunks):
            hid = _silu(branch(j)) * branch(n_chunks + j)
            hid_ref[rows, j * tf:(j + 1) * tf] = hid.reshape(tm, tf).astype(BF16)

        out = jnp.dot(hid_ref[rows, :], wdn_ref[...], preferred_element_type=F32)
        o_ref[rows, :] = x_ref[rows, :] + mod[:, 5 * d:6 * d] * out


def _ffn(x2, mods_i, ng, w_up, dw_w, dw_b, w_down, *, tokens_per_batch, row_len, tm, tf):
    n, d = x2.shape
    nj = D_FF // tf
    tile = pl.BlockSpec((tm, d), lambda t: (t, 0))
    w_up3 = w_up.reshape(d, 2 * nj, tf).transpose(1, 0, 2)
    dw_w3 = dw_w.reshape(3, 2 * nj, tf).transpose(1, 0, 2)
    dw_b3 = dw_b.reshape(2 * nj, 1, tf)
    return pl.pallas_call(
        functools.partial(_ffn_kernel, row_len=row_len, n_chunks=nj, n_sub=FFN_SUB_TILES),
        grid=(n // tm,),
        in_specs=[tile, _mod_spec(tokens_per_batch, tm), _const((1, d)),
                  _const((2 * nj, d, tf)), _const((2 * nj, 3, tf)), _const((2 * nj, 1, tf)),
                  _const((D_FF, d))],
        out_specs=tile,
        out_shape=jax.ShapeDtypeStruct((n, d), F32),
        scratch_shapes=[pltpu.VMEM((tm, d), BF16), pltpu.VMEM((tm, D_FF), BF16)],
        compiler_params=_params(("arbitrary",)),
        name="conv_ffn",
    )(x2, mods_i, ng, w_up3, dw_w3, dw_b3, w_down)


def _rope_tables(n_tok):
    t = jnp.arange(n_tok)
    row = (t // GRID_W).astype(F32)
    col = (t % GRID_W).astype(F32)
    half = DA_HEAD_DIM // 2
    inv = ROPE_THETA ** (-jnp.arange(0, half, 2, dtype=F32) / half)
    ang_r = row[:, None] * inv
    ang_c = col[:, None] * inv
    ang = jnp.concatenate([ang_r, ang_r, ang_c, ang_c], axis=-1)
    ang = jnp.concatenate([ang, ang], axis=-1)
    first = (jnp.arange(LANES) % (2 * (DA_HEAD_DIM // 4))) < DA_HEAD_DIM // 4
    sin = jnp.sin(ang)
    return jnp.cos(ang), jnp.where(first, -sin, 0.0), jnp.where(first, 0.0, sin)


def kernel(x, c, ctx, c_ctx, ada_w, ada_b, mix_norm_g, ffn_norm_g, cv_pw1_w, cv_pw1_b, cv_dw_w, cv_dw_b, cv_ln_g, cv_ln_b, cv_pw2_w, cv_pw2_b, da_wqkv, da_wo, da_qn_g, da_kn_g, da_lq1, da_lk1, da_lq2, da_lk2, da_subln_g, ffn_w_up, ffn_dw_w, ffn_dw_b, ffn_w_down):
    bsz, n_lat, d = x.shape
    n_ctx = ctx.shape[1]
    row = lambda a: a.reshape(1, -1)

    c_all = jnp.zeros((MOD_ROWS, d), F32).at[:bsz].set(c).at[CTX_MOD_ROW].set(c_ctx)
    mods = _mods(c_all, ada_w, ada_b)
    tables = _rope_tables(n_lat)
    gm = _group_mean_matrix()

    xl = x.reshape(bsz * n_lat, d)
    xc = ctx.reshape(bsz * n_ctx, d)
    lat = dict(tokens_per_batch=n_lat)
    cx = dict(tokens_per_batch=None)

    for i in range(DEPTH):
        last = i == DEPTH - 1
        j = i // N_MIXERS
        mods_i = mods[i].reshape(MOD_ROWS, 1, 6 * d)
        ng = row(mix_norm_g[i])
        if i % N_MIXERS == 0:
            cp = (ng, cv_pw1_w[j].astype(BF16), row(cv_pw1_b[j]), cv_dw_w[j], row(cv_dw_b[j]),
                  row(cv_ln_g[j]), row(cv_ln_b[j]), cv_pw2_w[j].astype(BF16), row(cv_pw2_b[j]))
            xl = _conv_mixer(xl, mods_i, *cp, row_len=GRID_W, tm=512, **lat)
            if not last:
                xc = _conv_mixer(xc, mods_i, *cp, row_len=n_ctx, tm=n_ctx, **cx)
        else:
            lam_init = 0.8 - 0.6 * math.exp(-0.3 * i)
            w = da_wqkv[j].astype(BF16)
            qg = row(jnp.tile(da_qn_g[j], MXU_TILE // DA_HEAD_DIM))
            kg = row(jnp.tile(da_kn_g[j], MXU_TILE // DA_HEAD_DIM))
            ap = (mods_i, row(da_lq1[j]), row(da_lk1[j]), row(da_lq2[j]), row(da_lk2[j]),
                  row(da_subln_g[j]), da_wo[j].astype(BF16))
            sh = lambda a, n: a.reshape(bsz, n, d)
            q_l, k_l, v_l = _qkv(xl, mods_i, ng, w, qg, kg, gm, tables, tm=512, **lat)
            q_c, k_c, v_c = _qkv(xc, mods_i, ng, w, qg, kg, gm, None, tm=512, **cx)
            kv_l = (sh(k_l, n_lat), sh(v_l, n_lat))
            kv_c = (sh(k_c, n_ctx), sh(v_c, n_ctx))
            xl = _attention(sh(q_l, n_lat), [kv_l, kv_c], xl, *ap, lam_init=lam_init, is_ctx=False,
                            tq=512, n_sub=1)
            if not last:
                xc = _attention(sh(q_c, n_ctx), [kv_c], xc, *ap, lam_init=lam_init, is_ctx=True,
                                tq=n_ctx, n_sub=1)

        fp = (row(ffn_norm_g[i]), ffn_w_up[i].astype(BF16), ffn_dw_w[i], row(ffn_dw_b[i]),
              ffn_w_down[i].astype(BF16))
        xl = _ffn(xl, mods_i, *fp, row_len=GRID_W, tm=1024, tf=256, **lat)
        if not last:
            xc = _ffn(xc, mods_i, *fp, row_len=n_ctx, tm=1024, tf=256, **cx)
    return xl.reshape(bsz, n_lat, d)
```
